```python
import math
import jax, jax.numpy as jnp
from jax import lax
import numpy as np

D_MODEL = 1024
BATCH = 8
SEQ = 4096
DEPTH = 2

HEAD_DIM = 64
HEADS_PER_GROUP = 8
DILATION_GROUPS = ((128, 1), (512, 4), (2048, 16))
N_GROUPS = len(DILATION_GROUPS)
ATTN_WIDTH = N_GROUPS * HEADS_PER_GROUP * HEAD_DIM
ATTN_OUT_WIDTH = HEADS_PER_GROUP * HEAD_DIM
CONV_CH = D_MODEL
CONV_WIDTH = 31
D_FF = ((8 * D_MODEL // 3 + 255) // 256) * 256
IN_WIDTH = 3 * ATTN_WIDTH + 2 * CONV_CH + 2 * D_MODEL
ROPE_THETA = 10000.0
Q_BLOCK = 128
EPS = 1e-6
NEG = -1e30

kernel_name = "hybrid_dilated_attn_conformer_conv_macaron"


def rmsnorm(x, g):
    xf = x.astype(jnp.float32)
    y = xf * lax.rsqrt(jnp.mean(xf * xf, axis=-1, keepdims=True) + EPS)
    return (y * g.astype(jnp.float32)).astype(x.dtype)


def swiglu_ffn(h, w_up, w_down):
    a, b = jnp.split(h @ w_up, 2, axis=-1)
    return (jax.nn.silu(a) * b) @ w_down


def rope_tables(seq):
    pos = jnp.arange(seq, dtype=jnp.float32)
    inv = ROPE_THETA ** (-jnp.arange(0, HEAD_DIM, 2, dtype=jnp.float32) / HEAD_DIM)
    ang = pos[:, None] * inv[None, :]
    return jnp.cos(ang), jnp.sin(ang)


def apply_rope(t, cos, sin):
    tf = t.astype(jnp.float32)
    t1, t2 = jnp.split(tf, 2, axis=-1)
    c = cos[:, None, None, :]
    s = sin[:, None, None, :]
    return jnp.concatenate([t1 * c - t2 * s, t1 * s + t2 * c], axis=-1).astype(t.dtype)


def dilated_window_attn(q, k, v, dil, w_sub):
    B, S, H, Dh = q.shape
    L = S // dil
    qb_size = min(Q_BLOCK, L)
    nb = -(-L // qb_size)
    Lp = nb * qb_size
    kb_size = qb_size + 2 * w_sub

    def to_sub(t):
        return t.reshape(B, L, dil, H, Dh).transpose(0, 2, 1, 3, 4)

    qs, ks, vs = to_sub(q), to_sub(k), to_sub(v)
    qs = jnp.pad(qs, ((0, 0), (0, 0), (0, Lp - L), (0, 0), (0, 0)))
    kpad = ((0, 0), (0, 0), (w_sub, w_sub + Lp - L), (0, 0), (0, 0))
    ks = jnp.pad(ks, kpad)
    vs = jnp.pad(vs, kpad)

    qblk = qs.reshape(B, dil, nb, qb_size, H, Dh)
    key_idx = jnp.arange(nb)[:, None] * qb_size + jnp.arange(kb_size)[None, :]
    kblk = ks[:, :, key_idx]
    vblk = vs[:, :, key_idx]

    scores = jnp.einsum('brnqhd,brnkhd->brnhqk', qblk, kblk,
                        preferred_element_type=jnp.float32) * (1.0 / math.sqrt(Dh))
    qpos = jnp.arange(nb)[:, None] * qb_size + jnp.arange(qb_size)[None, :]
    kpos = key_idx - w_sub
    valid = (jnp.abs(qpos[:, :, None] - kpos[:, None, :]) <= w_sub) \
        & (kpos[:, None, :] >= 0) & (kpos[:, None, :] < L)
    scores = jnp.where(valid[None, None, :, None], scores, NEG)

    m = jnp.max(scores, axis=-1, keepdims=True)
    p = jnp.exp(scores - m)
    denom = jnp.sum(p, axis=-1)
    out = jnp.einsum('brnhqk,brnkhd->brnqhd', p, vblk.astype(jnp.float32))
    out = out / jnp.moveaxis(denom, -1, -2)[..., None]
    lse = jnp.moveaxis(m[..., 0] + jnp.log(denom), -1, -2)

    out = out.reshape(B, dil, Lp, H, Dh)[:, :, :L].transpose(0, 2, 1, 3, 4).reshape(B, S, H, Dh)
    lse = lse.reshape(B, dil, Lp, H)[:, :, :L].transpose(0, 2, 1, 3).reshape(B, S, H)
    return out.astype(v.dtype), lse


def conv_module(glu_in, conv_w, conv_b, ln_g, ln_b, w_proj, b_proj):
    a, g = jnp.split(glu_in, 2, axis=-1)
    u = a * jax.nn.sigmoid(g)
    u = lax.conv_general_dilated(
        u, conv_w[:, None, :].astype(u.dtype), window_strides=(1,),
        padding=[(CONV_WIDTH // 2, CONV_WIDTH // 2)],
        dimension_numbers=('NWC', 'WIO', 'NWC'),
        feature_group_count=CONV_CH) + conv_b
    uf = u.astype(jnp.float32)
    mu = jnp.mean(uf, axis=-1, keepdims=True)
    var = jnp.mean(jnp.square(uf - mu), axis=-1, keepdims=True)
    u = ((uf - mu) * lax.rsqrt(var + EPS) * ln_g.astype(jnp.float32)
         + ln_b.astype(jnp.float32)).astype(u.dtype)
    return jax.nn.silu(u) @ w_proj + b_proj


def mixer(h, w_in, b_in, q_norm, k_norm, w_attn_proj, conv_w, conv_b, conv_ln_g,
          conv_ln_b, w_conv_proj, b_conv_proj, w_out, cos, sin):
    B, S, _ = h.shape
    z = h @ w_in + b_in
    q, k, v, glu_in, gates = jnp.split(
        z, [ATTN_WIDTH, 2 * ATTN_WIDTH, 3 * ATTN_WIDTH, 3 * ATTN_WIDTH + 2 * CONV_CH], axis=-1)
    shp = (B, S, N_GROUPS, HEADS_PER_GROUP, HEAD_DIM)
    q = apply_rope(rmsnorm(q.reshape(shp), q_norm[:, None, :]), cos, sin)
    k = apply_rope(rmsnorm(k.reshape(shp), k_norm[:, None, :]), cos, sin)
    v = v.reshape(shp)

    outs, lses = [], []
    for g, (window, dil) in enumerate(DILATION_GROUPS):
        o, l = dilated_window_attn(q[:, :, g], k[:, :, g], v[:, :, g], dil, window // (2 * dil))
        outs.append(o)
        lses.append(l)
    wts = jax.nn.softmax(jnp.stack(lses, axis=0), axis=0)
    y_attn = jnp.sum(wts[..., None] * jnp.stack(outs, axis=0).astype(jnp.float32), axis=0)
    y_attn = y_attn.astype(h.dtype).reshape(B, S, ATTN_OUT_WIDTH) @ w_attn_proj

    y_conv = conv_module(glu_in, conv_w, conv_b, conv_ln_g, conv_ln_b, w_conv_proj, b_conv_proj)

    g_attn, g_conv = jnp.split(jax.nn.sigmoid(gates), 2, axis=-1)
    return (g_attn * y_attn + g_conv * y_conv) @ w_out


def setup_inputs(seed: int = 0) -> dict:
    key = jax.random.key(seed)
    ks = jax.random.split(key, 24)
    f32 = jnp.float32
    L = DEPTH

    def w(k, shape, fan_in):
        return jax.random.normal(k, shape, f32) * (fan_in ** -0.5)

    def gain(k, shape):
        return 1.0 + 0.02 * jax.random.normal(k, shape, f32)

    def bias(k, shape):
        return 0.02 * jax.random.normal(k, shape, f32)

    return {
        "x": jax.random.normal(ks[0], (BATCH, SEQ, D_MODEL), f32),
        "ffn1_norm": gain(ks[1], (L, D_MODEL)),
        "ffn1_w_up": w(ks[2], (L, D_MODEL, 2 * D_FF), D_MODEL),
        "ffn1_w_down": w(ks[3], (L, D_FF, D_MODEL), D_FF),
        "mix_norm": gain(ks[4], (L, D_MODEL)),
        "w_in": w(ks[5], (L, D_MODEL, IN_WIDTH), D_MODEL),
        "b_in": bias(ks[6], (L, IN_WIDTH)),
        "q_norm": gain(ks[7], (L, N_GROUPS, HEAD_DIM)),
        "k_norm": gain(ks[8], (L, N_GROUPS, HEAD_DIM)),
        "w_attn_proj": w(ks[9], (L, ATTN_OUT_WIDTH, D_MODEL), ATTN_OUT_WIDTH),
        "conv_w": w(ks[10], (L, CONV_WIDTH, CONV_CH), CONV_WIDTH),
        "conv_b": bias(ks[11], (L, CONV_CH)),
        "conv_ln_g": gain(ks[12], (L, CONV_CH)),
        "conv_ln_b": bias(ks[13], (L, CONV_CH)),
        "w_conv_proj": w(ks[14], (L, CONV_CH, D_MODEL), CONV_CH),
        "b_conv_proj": bias(ks[15], (L, D_MODEL)),
        "w_out": w(ks[16], (L, D_MODEL, D_MODEL), D_MODEL),
        "ffn2_norm": gain(ks[17], (L, D_MODEL)),
        "ffn2_w_up": w(ks[18], (L, D_MODEL, 2 * D_FF), D_MODEL),
        "ffn2_w_down": w(ks[19], (L, D_FF, D_MODEL), D_FF),
        "final_norm": gain(ks[20], (L, D_MODEL)),
    }


def reference(x, ffn1_norm, ffn1_w_up, ffn1_w_down, mix_norm, w_in, b_in, q_norm, k_norm,
              w_attn_proj, conv_w, conv_b, conv_ln_g, conv_ln_b, w_conv_proj, b_conv_proj,
              w_out, ffn2_norm, ffn2_w_up, ffn2_w_down, final_norm):
    cos, sin = rope_tables(x.shape[1])
    for l in range(DEPTH):
        x = x + 0.5 * swiglu_ffn(rmsnorm(x, ffn1_norm[l]), ffn1_w_up[l], ffn1_w_down[l])
        x = x + mixer(rmsnorm(x, mix_norm[l]), w_in[l], b_in[l], q_norm[l], k_norm[l],
                      w_attn_proj[l], conv_w[l], conv_b[l], conv_ln_g[l], conv_ln_b[l],
                      w_conv_proj[l], b_conv_proj[l], w_out[l], cos, sin)
        x = x + 0.5 * swiglu_ffn(rmsnorm(x, ffn2_norm[l]), ffn2_w_up[l], ffn2_w_down[l])
        x = rmsnorm(x, final_norm[l])
    return x
```

```python
import functools
import math

import jax
import jax.numpy as jnp
from jax import lax
from jax.experimental import pallas as pl
from jax.experimental.pallas import tpu as pltpu

HEAD_DIM = 64
HEADS_PER_GROUP = 8
DILATION_GROUPS = ((128, 1), (512, 4), (2048, 16))
N_GROUPS = len(DILATION_GROUPS)
GROUP_WIDTH = HEADS_PER_GROUP * HEAD_DIM
ROPE_THETA = 10000.0
EPS = 1e-6
NEG = -1e30

LANES = 128
CHUNK = 256
VMEM_BYTES = 64 * 1024 * 1024
HALO = 16
Q_BLOCK = 128

F32 = jnp.float32
BF16 = jnp.bfloat16


def _rms(x, g):
    return x * lax.rsqrt(jnp.mean(x * x, axis=-1, keepdims=True) + EPS) * g


def _const_spec(shape):
    zeros = (0,) * len(shape)
    return pl.BlockSpec(shape, lambda *_: zeros, pipeline_mode=pl.Buffered(1))


def _params(n_axes, vmem_bytes):
    return pltpu.CompilerParams(
        dimension_semantics=("arbitrary",) * n_axes,
        vmem_limit_bytes=min(int(vmem_bytes), VMEM_BYTES - 4 * 1024 * 1024))


def _ffn_body(final_norm, x_ref, g_ref, wa_ref, wb_ref, wd_ref, *rest):
    if final_norm:
        fg_ref, o_ref, h_scr, acc_scr = rest
    else:
        o_ref, h_scr, acc_scr = rest
    x = x_ref[...]
    h_scr[...] = _rms(x, g_ref[...]).astype(BF16)
    acc_scr[...] = jnp.zeros_like(acc_scr)

    def chunk(c, carry):
        h = h_scr[...]
        a = jnp.dot(h, wa_ref[c], preferred_element_type=F32)
        b = jnp.dot(h, wb_ref[c], preferred_element_type=F32)
        s = (a * jax.nn.sigmoid(a) * b).astype(BF16)
        acc_scr[...] += jnp.dot(s, wd_ref[c], preferred_element_type=F32)
        return carry

    lax.fori_loop(0, wa_ref.shape[0], chunk, 0)
    y = x + 0.5 * acc_scr[...]
    if final_norm:
        y = _rms(y, fg_ref[...])
    o_ref[...] = y


def _ffn(x, norm_g, w_up, w_down, final_g, tm):
    t, d = x.shape
    d_ff = w_down.shape[0]
    n_chunks = d_ff // CHUNK
    w_up = w_up.astype(BF16).reshape(d, 2, n_chunks, CHUNK).transpose(1, 2, 0, 3)
    wa, wb = w_up[0], w_up[1]
    wd = w_down.astype(BF16).reshape(n_chunks, CHUNK, d)
    final_norm = final_g is not None
    args = [x, norm_g.reshape(1, d), wa, wb, wd]
    specs = [pl.BlockSpec((tm, d), lambda i: (i, 0)), _const_spec((1, d)),
             _const_spec(wa.shape), _const_spec(wb.shape), _const_spec(wd.shape)]
    if final_norm:
        args.append(final_g.reshape(1, d))
        specs.append(_const_spec((1, d)))
    vmem = (4 * tm * d * 4 + 3 * d_ff * d * 2 + tm * d * 6 + 8 * tm * CHUNK * 4 + (8 << 20))
    return pl.pallas_call(
        functools.partial(_ffn_body, final_norm),
        out_shape=jax.ShapeDtypeStruct((t, d), F32),
        grid=(t // tm,),
        in_specs=specs,
        out_specs=pl.BlockSpec((tm, d), lambda i: (i, 0)),
        scratch_shapes=[pltpu.VMEM((tm, d), BF16), pltpu.VMEM((tm, d), F32)],
        compiler_params=_params(1, vmem),
        name="ffn_final" if final_norm else "ffn",
    )(*args)


def _head_norm_rope(z, gain, cos, sin_signed):
    lane = lax.broadcasted_iota(jnp.int32, z.shape, 1)
    first_head = lane < HEAD_DIM
    z2 = z * z
    s_lo = jnp.sum(jnp.where(first_head, z2, 0.0), axis=-1, keepdims=True)
    s_hi = jnp.sum(jnp.where(first_head, 0.0, z2), axis=-1, keepdims=True)
    r = jnp.where(first_head, lax.rsqrt(s_lo * (1.0 / HEAD_DIM) + EPS),
                  lax.rsqrt(s_hi * (1.0 / HEAD_DIM) + EPS))
    t = z * r * gain
    half = HEAD_DIM // 2
    first_half = (lane % HEAD_DIM) < half
    partner = jnp.where(first_half, pltpu.roll(t, LANES - half, 1), pltpu.roll(t, half, 1))
    return t * cos + partner * sin_signed


def _proj_body(x_ref, g_ref, wq_ref, wk_ref, wv_ref, wa_ref, wg_ref, wt_ref,
               bq_ref, bk_ref, bv_ref, ba_ref, bg_ref, bt_ref, gq_ref, gk_ref,
               cos_ref, sin_ref,
               q_ref, k_ref, v_ref, u_ref, t_ref, h_scr):
    h_scr[...] = _rms(x_ref[...], g_ref[...]).astype(BF16)

    def qk_chunk(w_ref, b_ref, gain_ref, o_ref, scale):
        def body(c, carry):
            z = jnp.dot(h_scr[...], w_ref[c], preferred_element_type=F32) + b_ref[c]
            gain = gain_ref[c]
            tiles = []
            for j in range(CHUNK // LANES):
                sl = slice(j * LANES, (j + 1) * LANES)
                tiles.append(_head_norm_rope(z[:, sl], gain[:, sl], cos_ref[...], sin_ref[...]))
            o_ref[c] = (jnp.concatenate(tiles, axis=1) * scale).astype(o_ref.dtype)
            return carry
        lax.fori_loop(0, w_ref.shape[0], body, 0)

    qk_chunk(wq_ref, bq_ref, gq_ref, q_ref, 1.0 / math.sqrt(HEAD_DIM))
    qk_chunk(wk_ref, bk_ref, gk_ref, k_ref, 1.0)

    def v_chunk(c, carry):
        z = jnp.dot(h_scr[...], wv_ref[c], preferred_element_type=F32) + bv_ref[c]
        v_ref[c] = z.astype(v_ref.dtype)
        return carry
    lax.fori_loop(0, wv_ref.shape[0], v_chunk, 0)

    def glu_chunk(c, carry):
        h = h_scr[...]
        a = jnp.dot(h, wa_ref[c], preferred_element_type=F32) + ba_ref[c]
        g = jnp.dot(h, wg_ref[c], preferred_element_type=F32) + bg_ref[c]
        u_ref[c] = a * jax.nn.sigmoid(g)
        return carry
    lax.fori_loop(0, wa_ref.shape[0], glu_chunk, 0)

    def gate_chunk(c, carry):
        z = jnp.dot(h_scr[...], wt_ref[c], preferred_element_type=F32) + bt_ref[c]
        t_ref[c] = jax.nn.sigmoid(z)
        return carry
    lax.fori_loop(0, wt_ref.shape[0], gate_chunk, 0)


def _chunk_cols(w, b):
    d, n = w.shape
    return (w.astype(BF16).reshape(d, n // CHUNK, CHUNK).transpose(1, 0, 2),
            b.reshape(n // CHUNK, 1, CHUNK))


def _head_gain(g):
    full = jnp.broadcast_to(g[:, None, :], (N_GROUPS, HEADS_PER_GROUP, HEAD_DIM))
    return full.reshape(N_GROUPS * GROUP_WIDTH // CHUNK, 1, CHUNK)


def _rope_tiles(seq):
    pos = jnp.arange(seq, dtype=F32)
    inv = ROPE_THETA ** (-jnp.arange(0, HEAD_DIM, 2, dtype=F32) / HEAD_DIM)
    ang = pos[:, None] * inv[None, :]
    cos, sin = jnp.cos(ang), jnp.sin(ang)
    reps = LANES // HEAD_DIM
    return (jnp.tile(jnp.concatenate([cos, cos], axis=1), (1, reps)),
            jnp.tile(jnp.concatenate([-sin, sin], axis=1), (1, reps)))


def _proj(x, mix_g, w_in, b_in, q_norm, k_norm, cos, sin, seq, tm):
    t, d = x.shape
    aw = N_GROUPS * GROUP_WIDTH
    conv_ch = (w_in.shape[1] - 3 * aw - 2 * d) // 2
    bounds = [0, aw, 2 * aw, 3 * aw, 3 * aw + conv_ch, 3 * aw + 2 * conv_ch, w_in.shape[1]]
    ws, bs = [], []
    for lo, hi in zip(bounds[:-1], bounds[1:]):
        w, b = _chunk_cols(w_in[:, lo:hi], b_in[lo:hi])
        ws.append(w)
        bs.append(b)
    gq, gk = _head_gain(q_norm), _head_gain(k_norm)
    seq_tiles = seq // tm
    args = [x, mix_g.reshape(1, d), *ws, *bs, gq, gk, cos, sin]
    specs = ([pl.BlockSpec((tm, d), lambda i: (i, 0)), _const_spec((1, d))]
             + [_const_spec(a.shape) for a in (*ws, *bs, gq, gk)]
             + [pl.BlockSpec((tm, LANES), lambda i: (i % seq_tiles, 0))] * 2)
    outs = [(ws[0].shape[0], BF16), (ws[1].shape[0], BF16), (ws[2].shape[0], BF16),
            (ws[3].shape[0], F32), (ws[5].shape[0], F32)]
    out_shape = [jax.ShapeDtypeStruct((n, t, CHUNK), dt) for n, dt in outs]
    out_specs = [pl.BlockSpec((n, tm, CHUNK), lambda i: (0, i, 0)) for n, _ in outs]
    out_bytes = sum(n * tm * CHUNK * jnp.dtype(dt).itemsize for n, dt in outs)
    vmem = (2 * tm * d * 4 + w_in.size * 2 + 2 * out_bytes + tm * d * 2
            + 4 * tm * LANES * 4 + 8 * tm * CHUNK * 4 + (8 << 20))
    return pl.pallas_call(
        _proj_body,
        out_shape=out_shape,
        grid=(t // tm,),
        in_specs=specs,
        out_specs=out_specs,
        scratch_shapes=[pltpu.VMEM((tm, d), BF16)],
        compiler_params=_params(1, vmem),
        name="in_proj",
    )(*args)


def _attn_body(sub_len, half_window, q_ref, k_ref, v_ref, o_ref, l_ref):
    qt_rows = q_ref.shape[1]
    key_rows = Q_BLOCK + 2 * half_window
    tile_start = pl.program_id(2) * qt_rows
    heads_per_chunk = CHUNK // HEAD_DIM
    for i in range(qt_rows // Q_BLOCK):
        q0 = tile_start + i * Q_BLOCK
        start = pl.multiple_of(jnp.clip(q0 - half_window, 0, sub_len - key_rows), half_window)
        qpos = q0 + lax.broadcasted_iota(jnp.int32, (Q_BLOCK, key_rows), 0)
        kpos = start + lax.broadcasted_iota(jnp.int32, (Q_BLOCK, key_rows), 1)
        valid = jnp.abs(qpos - kpos) <= half_window
        rows = slice(i * Q_BLOCK, (i + 1) * Q_BLOCK)
        for c in range(q_ref.shape[0]):
            qb = q_ref[c, rows, :]
            kw = k_ref[c, pl.ds(start, key_rows), :]
            vw = v_ref[c, pl.ds(start, key_rows), :]
            for hh in range(heads_per_chunk):
                sl = slice(hh * HEAD_DIM, (hh + 1) * HEAD_DIM)
                s = lax.dot_general(qb[:, sl], kw[:, sl], (((1,), (1,)), ((), ())),
                                    preferred_element_type=F32)
                s = jnp.where(valid, s, NEG)
                m = jnp.max(s, axis=-1, keepdims=True)
                p = jnp.exp(s - m)
                den = jnp.sum(p, axis=-1, keepdims=True)
                o = jnp.dot(p.astype(BF16), vw[:, sl], preferred_element_type=F32) / den
                cols = slice(c * CHUNK + hh * HEAD_DIM, c * CHUNK + (hh + 1) * HEAD_DIM)
                o_ref[rows, cols] = o
                l_ref[rows, cols] = jnp.broadcast_to(m + jnp.log(den), (Q_BLOCK, HEAD_DIM))


def _attn_group(q, k, v, g, batch, seq):
    window, dil = DILATION_GROUPS[g]
    half_window = window // (2 * dil)
    sub_len = seq // dil
    qt = min(512, sub_len)
    n_chunks = q.shape[0]
    cpg = GROUP_WIDTH // CHUNK
    view = lambda a: a.reshape(n_chunks, batch, sub_len, dil * CHUNK)
    q_spec = pl.BlockSpec((cpg, None, qt, CHUNK), lambda b, r, t: (g, b, t, r))
    kv_spec = pl.BlockSpec((cpg, None, sub_len, CHUNK), lambda b, r, t: (g, b, 0, r))
    o_spec = pl.BlockSpec((None, qt, GROUP_WIDTH), lambda b, r, t: (b, t, r))
    o_shape = jax.ShapeDtypeStruct((batch, sub_len, dil * GROUP_WIDTH), F32)
    vmem = (2 * cpg * qt * CHUNK * 2 + 4 * cpg * sub_len * CHUNK * 2
            + 4 * qt * GROUP_WIDTH * 4 + (8 << 20))
    o, l = pl.pallas_call(
        functools.partial(_attn_body, sub_len, half_window),
        out_shape=[o_shape, o_shape],
        grid=(batch, dil, sub_len // qt),
        in_specs=[q_spec, kv_spec, kv_spec],
        out_specs=[o_spec, o_spec],
        compiler_params=_params(3, vmem),
        name=f"attn_d{dil}",
    )(view(q), view(k), view(v))
    return o.reshape(batch * seq, GROUP_WIDTH), l.reshape(batch * seq, GROUP_WIDTH)


def _mix_body(conv_width, o0_ref, o1_ref, o2_ref, l0_ref, l1_ref, l2_ref,
              u_ref, up_ref, un_ref, t_ref, x_ref,
              wap_ref, cw_ref, cb_ref, lg_ref, lb_ref, wcp_ref, bcp_ref, wo_ref,
              out_ref, ue_scr, cv_scr):
    i = pl.program_id(1)
    ts = x_ref.shape[0]
    n_cc = u_ref.shape[0]
    d = x_ref.shape[1]

    l0, l1, l2 = l0_ref[...], l1_ref[...], l2_ref[...]
    m = jnp.maximum(jnp.maximum(l0, l1), l2)
    e0, e1, e2 = jnp.exp(l0 - m), jnp.exp(l1 - m), jnp.exp(l2 - m)
    y = (e0 * o0_ref[...] + e1 * o1_ref[...] + e2 * o2_ref[...]) / (e0 + e1 + e2)
    y_attn = jnp.dot(y.astype(BF16), wap_ref[...], preferred_element_type=F32)

    keep_prev = jnp.where(i > 0, 1.0, 0.0)
    keep_next = jnp.where(i < pl.num_programs(1) - 1, 1.0, 0.0)
    ue_scr[:, 0:HALO, :] = up_ref[...] * keep_prev
    ue_scr[:, HALO:HALO + ts, :] = u_ref[...]
    ue_scr[:, HALO + ts:, :] = un_ref[...] * keep_next
    pad = conv_width // 2
    row_sum = jnp.zeros((ts, 1), F32)
    for c in range(n_cc):
        cols = slice(c * CHUNK, (c + 1) * CHUNK)
        acc = jnp.zeros((ts, CHUNK), F32) + cb_ref[:, cols]
        for j in range(conv_width):
            lo = HALO - pad + j
            acc = acc + ue_scr[c, lo:lo + ts, :] * cw_ref[j:j + 1, cols]
        cv_scr[c] = acc
        row_sum = row_sum + jnp.sum(acc, axis=-1, keepdims=True)
    mu = row_sum * (1.0 / d)
    sq_sum = jnp.zeros((ts, 1), F32)
    for c in range(n_cc):
        dev = cv_scr[c] - mu
        sq_sum = sq_sum + jnp.sum(dev * dev, axis=-1, keepdims=True)
    rstd = lax.rsqrt(sq_sum * (1.0 / d) + EPS)
    acts = []
    for c in range(n_cc):
        cols = slice(c * CHUNK, (c + 1) * CHUNK)
        ln = (cv_scr[c] - mu) * rstd * lg_ref[:, cols] + lb_ref[:, cols]
        acts.append((ln * jax.nn.sigmoid(ln)).astype(BF16))
    y_conv = jnp.dot(jnp.concatenate(acts, axis=1), wcp_ref[...],
                     preferred_element_type=F32) + bcp_ref[...]

    g_attn = jnp.concatenate([t_ref[c] for c in range(n_cc)], axis=1)
    g_conv = jnp.concatenate([t_ref[n_cc + c] for c in range(n_cc)], axis=1)
    merged = (g_attn * y_attn + g_conv * y_conv).astype(BF16)
    out_ref[...] = x_ref[...] + jnp.dot(merged, wo_ref[...], preferred_element_type=F32)


def _mix_out(x, outs, lses, u, gates, w_attn_proj, conv_w, conv_b, ln_g, ln_b,
             w_conv_proj, b_conv_proj, w_out, batch, seq, ts):
    t, d = x.shape
    n_cc = u.shape[0]
    conv_width = conv_w.shape[0]
    tiles = seq // ts
    hb = ts // HALO
    last_hb = seq // HALO - 1
    row = lambda b, i: (b * tiles + i, 0)
    ol_spec = pl.BlockSpec((ts, GROUP_WIDTH), row)
    u_spec = pl.BlockSpec((n_cc, ts, CHUNK), lambda b, i: (0, b * tiles + i, 0))
    up_spec = pl.BlockSpec((n_cc, HALO, CHUNK),
                           lambda b, i: (0, b * (last_hb + 1) + jnp.maximum(i * hb - 1, 0), 0))
    un_spec = pl.BlockSpec((n_cc, HALO, CHUNK),
                           lambda b, i: (0, b * (last_hb + 1) + jnp.minimum((i + 1) * hb, last_hb), 0))
    t_spec = pl.BlockSpec((gates.shape[0], ts, CHUNK), lambda b, i: (0, b * tiles + i, 0))
    weights = [w_attn_proj.astype(BF16), conv_w, conv_b.reshape(1, d), ln_g.reshape(1, d),
               ln_b.reshape(1, d), w_conv_proj.astype(BF16), b_conv_proj.reshape(1, d),
               w_out.astype(BF16)]
    vmem = (2 * 6 * ts * GROUP_WIDTH * 4 + 2 * (ts + 2 * HALO) * d * 4 + 2 * 2 * ts * d * 4
            + 4 * ts * d * 4 + (GROUP_WIDTH + 2 * d) * d * 2 + 2 * (ts + 2 * HALO) * d * 4
            + 12 * ts * d * 4 + (8 << 20))
    return pl.pallas_call(
        functools.partial(_mix_body, conv_width),
        out_shape=jax.ShapeDtypeStruct((t, d), F32),
        grid=(batch, tiles),
        in_specs=[ol_spec] * 6 + [u_spec, up_spec, un_spec, t_spec, pl.BlockSpec((ts, d), row)]
                 + [_const_spec(w.shape) for w in weights],
        out_specs=pl.BlockSpec((ts, d), row),
        scratch_shapes=[pltpu.VMEM((n_cc, ts + 2 * HALO, CHUNK), F32),
                        pltpu.VMEM((n_cc, ts, CHUNK), F32)],
        compiler_params=_params(2, vmem),
        name="mix_out",
    )(*outs, *lses, u, u, u, gates, x, *weights)


def kernel(x, ffn1_norm, ffn1_w_up, ffn1_w_down, mix_norm, w_in, b_in, q_norm, k_norm,
           w_attn_proj, conv_w, conv_b, conv_ln_g, conv_ln_b, w_conv_proj, b_conv_proj,
           w_out, ffn2_norm, ffn2_w_up, ffn2_w_down, final_norm):
    batch, seq, d = x.shape
    depth = ffn1_norm.shape[0]
    cos, sin = _rope_tiles(seq)
    h = x.reshape(batch * seq, d)
    for l in range(depth):
        h = _ffn(h, ffn1_norm[l], ffn1_w_up[l], ffn1_w_down[l], None, tm=512)
        q, k, v, u, gates = _proj(h, mix_norm[l], w_in[l], b_in[l], q_norm[l], k_norm[l],
                                  cos, sin, seq, tm=512)
        outs, lses = zip(*[_attn_group(q, k, v, g, batch, seq) for g in range(N_GROUPS)])
        h = _mix_out(h, outs, lses, u, gates, w_attn_proj[l], conv_w[l], conv_b[l],
                     conv_ln_g[l], conv_ln_b[l], w_conv_proj[l], b_conv_proj[l], w_out[l],
                     batch, seq, ts=256)
        h = _ffn(h, ffn2_norm[l], ffn2_w_up[l], ffn2_w_down[l], final_norm[l], tm=512)
    return h.reshape(batch, seq, d)
```

```python
import functools
import math

import numpy as np

import jax
import jax.numpy as jnp
from jax import lax
from jax.experimental import pallas as pl
from jax.experimental.pallas import tpu as pltpu

HEAD_DIM = 64
HEADS_PER_GROUP = 8
DILATION_GROUPS = ((128, 1), (512, 4), (2048, 16))
N_GROUPS = len(DILATION_GROUPS)
GROUP_WIDTH = HEADS_PER_GROUP * HEAD_DIM
ROPE_THETA = 10000.0
EPS = 1e-6
NEG = -1e30

LANES = 128
CHUNK = 256
VMEM_BYTES = 64 * 1024 * 1024
HALO = 16
Q_BLOCK = 128
CONV_ROWS = 128
HEADS_PER_CHUNK = CHUNK // HEAD_DIM
CHUNKS_PER_GROUP = GROUP_WIDTH // CHUNK
HALF = HEAD_DIM // 2

F32 = jnp.float32
BF16 = jnp.bfloat16


def _rms(x, g):
    return x * lax.rsqrt(jnp.mean(x * x, axis=-1, keepdims=True) + EPS) * g


def _const_spec(shape):
    zeros = (0,) * len(shape)
    return pl.BlockSpec(shape, lambda *_: zeros, pipeline_mode=pl.Buffered(1))


def _params(n_axes, vmem_bytes):
    return pltpu.CompilerParams(
        dimension_semantics=("arbitrary",) * n_axes,
        vmem_limit_bytes=min(int(vmem_bytes), VMEM_BYTES - 4 * 1024 * 1024))


def _ffn_body(final_norm, x_ref, g_ref, wa_ref, wb_ref, wd_ref, *rest):
    if final_norm:
        fg_ref, o_ref, h_scr, acc_scr = rest
    else:
        o_ref, h_scr, acc_scr = rest
    h_scr[...] = _rms(x_ref[...], g_ref[...]).astype(BF16)
    acc_scr[...] = jnp.zeros_like(acc_scr)

    def chunk(c, carry):
        h = h_scr[...]
        a = jnp.dot(h, wa_ref[c], preferred_element_type=F32)
        b = jnp.dot(h, wb_ref[c], preferred_element_type=F32)
        s = (a * jax.nn.sigmoid(a) * b).astype(BF16)
        acc_scr[...] += jnp.dot(s, wd_ref[c], preferred_element_type=F32)
        return carry

    lax.fori_loop(0, wa_ref.shape[0], chunk, 0)
    y = x_ref[...] + 0.5 * acc_scr[...]
    if final_norm:
        y = _rms(y, fg_ref[...])
    o_ref[...] = y


def _ffn(x, norm_g, w_up, w_down, final_g, tm):
    t, d = x.shape
    d_ff = w_down.shape[0]
    n_chunks = d_ff // CHUNK
    w_up = w_up.astype(BF16).reshape(d, 2, n_chunks, CHUNK).transpose(1, 2, 0, 3)
    wa, wb = w_up[0], w_up[1]
    wd = w_down.astype(BF16).reshape(n_chunks, CHUNK, d)
    final_norm = final_g is not None
    args = [x, norm_g.reshape(1, d), wa, wb, wd]
    specs = [pl.BlockSpec((tm, d), lambda i: (i, 0)), _const_spec((1, d)),
             _const_spec(wa.shape), _const_spec(wb.shape), _const_spec(wd.shape)]
    if final_norm:
        args.append(final_g.reshape(1, d))
        specs.append(_const_spec((1, d)))
    vmem = (4 * tm * d * 4 + 3 * d_ff * d * 2 + tm * d * 6 + 8 * tm * CHUNK * 4 + (8 << 20))
    return pl.pallas_call(
        functools.partial(_ffn_body, final_norm),
        out_shape=jax.ShapeDtypeStruct((t, d), F32),
        grid=(t // tm,),
        in_specs=specs,
        out_specs=pl.BlockSpec((tm, d), lambda i: (i, 0)),
        scratch_shapes=[pltpu.VMEM((tm, d), BF16), pltpu.VMEM((tm, d), F32)],
        compiler_params=_params(1, vmem),
        name="ffn_final" if final_norm else "ffn",
    )(*args)


def _head_norm_rope(z, seg_ref, gain, cos, sin):
    z0, z1 = z[:, :LANES], z[:, LANES:]
    w = z0 * z0 + z1 * z1
    w_hi = w.astype(BF16)
    w_lo = (w - w_hi.astype(F32)).astype(BF16)
    ss = jnp.dot(jnp.concatenate([w_hi, w_lo], axis=1), seg_ref[...], preferred_element_type=F32)
    r = lax.rsqrt(ss * (1.0 / HEAD_DIM) + EPS)
    t0 = z0 * r * gain[:, :LANES]
    t1 = z1 * r * gain[:, LANES:]
    return jnp.concatenate([t0 * cos - t1 * sin, t1 * cos + t0 * sin], axis=1)


def _proj_body(dils, x_ref, g_ref, wq_ref, wk_ref, wv_ref, wa_ref, wg_ref, wt_ref,
               bq_ref, bk_ref, bv_ref, ba_ref, bg_ref, bt_ref, gq_ref, gk_ref,
               rope_ref, seg_ref, *rest):
    qkv_refs = rest[:3 * N_GROUPS]
    u_ref, t_ref, hn_scr = rest[3 * N_GROUPS:3 * N_GROUPS + 3]
    h_scrs = rest[3 * N_GROUPS + 3:]
    tm = x_ref.shape[0]

    hn = _rms(x_ref[...], g_ref[...])
    h_scrs[0][...] = hn.astype(BF16)
    lane_tiles = hn_scr.shape[0] // tm
    for j in range(lane_tiles):
        hn_scr[j * tm:(j + 1) * tm, :] = hn[:, j * LANES:(j + 1) * LANES]
    for dil, h_scr in zip(dils[1:], h_scrs[1:]):
        rows = tm // dil
        for r in range(dil):
            for j in range(lane_tiles):
                src = hn_scr[pl.ds(j * tm + r, rows, stride=dil), :]
                h_scr[r * rows:(r + 1) * rows, j * LANES:(j + 1) * LANES] = src.astype(BF16)

    for g, dil in enumerate(dils):
        q_ref, k_ref, v_ref = qkv_refs[3 * g:3 * g + 3]
        h = h_scrs[g][...]
        shape = (dil, tm // dil, CHUNK)
        for c in range(CHUNKS_PER_GROUP):
            n = g * CHUNKS_PER_GROUP + c
            zq = jnp.dot(h, wq_ref[n], preferred_element_type=F32) + bq_ref[n]
            q = _head_norm_rope(zq, seg_ref, gq_ref[n], rope_ref[g, 0], rope_ref[g, 1])
            q_ref[c] = q.astype(BF16).reshape(shape)
            zk = jnp.dot(h, wk_ref[n], preferred_element_type=F32) + bk_ref[n]
            k = _head_norm_rope(zk, seg_ref, gk_ref[n], rope_ref[g, 2], rope_ref[g, 3])
            k_ref[c] = k.astype(BF16).reshape(shape)
            zv = jnp.dot(h, wv_ref[n], preferred_element_type=F32) + bv_ref[n]
            v_ref[c] = zv.astype(BF16).reshape(shape)

    h = h_scrs[0][...]
    for c in range(wa_ref.shape[0]):
        a = jnp.dot(h, wa_ref[c], preferred_element_type=F32) + ba_ref[c]
        gl = jnp.dot(h, wg_ref[c], preferred_element_type=F32) + bg_ref[c]
        u = a * jax.nn.sigmoid(gl)
        for j in range(CHUNK // LANES):
            u_ref[c * (CHUNK // LANES) + j] = u[:, j * LANES:(j + 1) * LANES]
    for c in range(wt_ref.shape[0]):
        z = jnp.dot(h, wt_ref[c], preferred_element_type=F32) + bt_ref[c]
        t_ref[c] = jax.nn.sigmoid(z)


def _rotary_perm(n_cols):
    new = np.arange(n_cols)
    chunk, lane = new // CHUNK, new % CHUNK
    tile, head, i = lane // LANES, (lane % LANES) // HALF, lane % HALF
    return chunk * CHUNK + head * HEAD_DIM + tile * HALF + i


def _chunk_cols(w, b):
    d, n = w.shape
    return (w.astype(BF16).reshape(d, n // CHUNK, CHUNK).transpose(1, 0, 2),
            b.reshape(n // CHUNK, 1, CHUNK))


def _head_gain(g, perm):
    full = jnp.broadcast_to(g[:, None, :], (N_GROUPS, HEADS_PER_GROUP, HEAD_DIM))
    return full.reshape(-1)[perm].reshape(-1, 1, CHUNK)


def _rope_tables(seq, tm, dils):
    inv = ROPE_THETA ** (-jnp.arange(0, HEAD_DIM, 2, dtype=F32) / HEAD_DIM)
    tables = []
    for dil in dils:
        row = np.arange(seq)
        tile, r, m = row // tm, (row % tm) // (tm // dil), row % (tm // dil)
        pos = jnp.asarray(tile * tm + m * dil + r, F32)
        ang = pos[:, None] * inv[None, :]
        cos = jnp.tile(jnp.cos(ang), (1, LANES // HALF))
        sin = jnp.tile(jnp.sin(ang), (1, LANES // HALF))
        scale = 1.0 / math.sqrt(HEAD_DIM)
        tables.append(jnp.stack([cos * scale, sin * scale, cos, sin]))
    return jnp.stack(tables)


def _segment_matrix():
    lane = np.arange(LANES)
    same = (lane[:, None] // HALF == lane[None, :] // HALF).astype(np.float32)
    return jnp.asarray(np.concatenate([same, same], axis=0), BF16)


def _proj(x, mix_g, w_in, b_in, q_norm, k_norm, rope, batch, seq, tm):
    t, d = x.shape
    aw = N_GROUPS * GROUP_WIDTH
    conv_ch = (w_in.shape[1] - 3 * aw - 2 * d) // 2
    dils = tuple(dil for _, dil in DILATION_GROUPS)
    perm = _rotary_perm(aw)
    bounds = [0, aw, 2 * aw, 3 * aw, 3 * aw + conv_ch, 3 * aw + 2 * conv_ch, w_in.shape[1]]
    ws, bs = [], []
    for n, (lo, hi) in enumerate(zip(bounds[:-1], bounds[1:])):
        w, b = w_in[:, lo:hi], b_in[lo:hi]
        if n < 2:
            w, b = w[:, perm], b[perm]
        w, b = _chunk_cols(w, b)
        ws.append(w)
        bs.append(b)
    gq, gk = _head_gain(q_norm, perm), _head_gain(k_norm, perm)
    seg = _segment_matrix()
    tiles = seq // tm
    args = [x, mix_g.reshape(1, d), *ws, *bs, gq, gk, rope, seg]
    specs = ([pl.BlockSpec((tm, d), lambda b, i: (b * tiles + i, 0)), _const_spec((1, d))]
             + [_const_spec(a.shape) for a in (*ws, *bs, gq, gk)]
             + [pl.BlockSpec((N_GROUPS, 4, tm, LANES), lambda b, i: (0, 0, i, 0)),
                _const_spec(seg.shape)])
    out_shape, out_specs = [], []
    for dil in dils:
        shape = (CHUNKS_PER_GROUP, batch, dil, seq // dil, CHUNK)
        spec = pl.BlockSpec((CHUNKS_PER_GROUP, None, dil, tm // dil, CHUNK),
                            lambda b, i: (0, b, 0, i, 0))
        out_shape += [jax.ShapeDtypeStruct(shape, BF16)] * 3
        out_specs += [spec] * 3
    n_u, n_t = conv_ch // LANES, ws[5].shape[0]
    out_shape += [jax.ShapeDtypeStruct((n_u, t, LANES), F32),
                  jax.ShapeDtypeStruct((n_t, t, CHUNK), F32)]
    out_specs += [pl.BlockSpec((n_u, tm, LANES), lambda b, i: (0, b * tiles + i, 0)),
                  pl.BlockSpec((n_t, tm, CHUNK), lambda b, i: (0, b * tiles + i, 0))]
    out_bytes = 3 * aw * tm * 2 + conv_ch * tm * 4 + n_t * CHUNK * tm * 4
    vmem = (2 * tm * d * 4 + w_in.size * 2 + 2 * out_bytes + tm * d * (4 + 2 * N_GROUPS)
            + 2 * N_GROUPS * 4 * tm * LANES * 4 + 8 * tm * CHUNK * 4 + (6 << 20))
    return pl.pallas_call(
        functools.partial(_proj_body, dils),
        out_shape=out_shape,
        grid=(batch, tiles),
        in_specs=specs,
        out_specs=out_specs,
        scratch_shapes=[pltpu.VMEM((d // LANES * tm, LANES), F32)] + [pltpu.VMEM((tm, d), BF16)] * N_GROUPS,
        compiler_params=_params(2, vmem),
        name="in_proj",
    )(*args)


def _attn_body(sub_len, half_window, q_ref, k_ref, v_ref, o_ref, l_ref):
    qt_rows = q_ref.shape[1]
    key_rows = Q_BLOCK + 2 * half_window
    tile_start = pl.program_id(2) * qt_rows
    lane = lax.broadcasted_iota(jnp.int32, (1, CHUNK), 1)
    head_lanes = [((lane % LANES) // HALF == hh).astype(BF16) for hh in range(HEADS_PER_CHUNK)]
    v_head = lax.broadcasted_iota(jnp.int32, (Q_BLOCK, CHUNK), 1) // HEAD_DIM
    offset = (lax.broadcasted_iota(jnp.int32, (Q_BLOCK, key_rows), 0)
              - lax.broadcasted_iota(jnp.int32, (Q_BLOCK, key_rows), 1))

    def block(i, carry):
        r0 = pl.multiple_of(i * Q_BLOCK, Q_BLOCK)
        q0 = tile_start + r0
        start = pl.multiple_of(jnp.clip(q0 - half_window, 0, sub_len - key_rows), half_window)
        valid = jnp.abs(offset + (q0 - start)) <= half_window
        for c in range(CHUNKS_PER_GROUP):
            qb = q_ref[c, pl.ds(r0, Q_BLOCK), :]
            kw = k_ref[c, pl.ds(start, key_rows), :]
            vw = v_ref[c, pl.ds(start, key_rows), :]
            qs = jnp.concatenate([qb * head_lanes[hh] for hh in range(HEADS_PER_CHUNK)], axis=0)
            s = lax.dot_general(qs, kw, (((1,), (1,)), ((), ())), preferred_element_type=F32)
            ps, dens, lses = [], [], []
            for hh in range(HEADS_PER_CHUNK):
                sh = jnp.where(valid, s[hh * Q_BLOCK:(hh + 1) * Q_BLOCK], NEG)
                m = jnp.max(sh, axis=-1, keepdims=True)
                p = jnp.exp(sh - m)
                den = jnp.sum(p, axis=-1, keepdims=True)
                ps.append(p.astype(BF16))
                dens.append(den)
                lses.append(m + jnp.log(den))
            pv = jnp.dot(jnp.concatenate(ps, axis=0), vw, preferred_element_type=F32)
            o, dn, ls = pv[:Q_BLOCK], dens[0], lses[0]
            for hh in range(1, HEADS_PER_CHUNK):
                mine = v_head == hh
                o = jnp.where(mine, pv[hh * Q_BLOCK:(hh + 1) * Q_BLOCK], o)
                dn = jnp.where(mine, dens[hh], dn)
                ls = jnp.where(mine, lses[hh], ls)
            cols = slice(c * CHUNK, (c + 1) * CHUNK)
            o_ref[pl.ds(r0, Q_BLOCK), cols] = o / dn
            l_ref[pl.ds(r0, Q_BLOCK), cols] = jnp.broadcast_to(ls, (Q_BLOCK, CHUNK))
        return carry

    lax.fori_loop(0, qt_rows // Q_BLOCK, block, 0)


def _attn_group(q, k, v, g):
    window, dil = DILATION_GROUPS[g]
    half_window = window // (2 * dil)
    _, batch, _, sub_len, _ = q.shape
    qt = min(512, sub_len)
    q_spec = pl.BlockSpec((CHUNKS_PER_GROUP, None, None, qt, CHUNK), lambda b, r, t: (0, b, r, t, 0))
    kv_spec = pl.BlockSpec((CHUNKS_PER_GROUP, None, None, sub_len, CHUNK),
                           lambda b, r, t: (0, b, r, 0, 0))
    o_spec = pl.BlockSpec((None, None, qt, GROUP_WIDTH), lambda b, r, t: (b, r, t, 0))
    o_shape = jax.ShapeDtypeStruct((batch, dil, sub_len, GROUP_WIDTH), F32)
    vmem = (2 * CHUNKS_PER_GROUP * qt * CHUNK * 2 + 4 * CHUNKS_PER_GROUP * sub_len * CHUNK * 2
            + 4 * qt * GROUP_WIDTH * 4 + (8 << 20))
    return pl.pallas_call(
        functools.partial(_attn_body, sub_len, half_window),
        out_shape=[o_shape, o_shape],
        grid=(batch, dil, sub_len // qt),
        in_specs=[q_spec, kv_spec, kv_spec],
        out_specs=[o_spec, o_spec],
        compiler_params=_params(3, vmem),
        name=f"attn_d{dil}",
    )(q, k, v)


def _mix_body(conv_width, dils, *refs):
    ol_refs = refs[:2 * N_GROUPS]
    (u_ref, up_ref, un_ref, t_ref, x_ref,
     wap_ref, cw_ref, cb_ref, lg_ref, lb_ref, wcp_ref, bcp_ref, wo_ref,
     out_ref, ue_scr, cv_scr, nat_scr) = refs[2 * N_GROUPS:]
    i = pl.program_id(1)
    ts, d = x_ref.shape
    n_ct = u_ref.shape[0]

    lane_tiles = GROUP_WIDTH // LANES
    natural = []
    for n, (ref, dil) in enumerate(zip(ol_refs, dils + dils)):
        if dil == 1:
            natural.append(ref[0])
            continue
        rows = ts // dil
        for r in range(dil):
            for j in range(lane_tiles):
                base = (n * lane_tiles + j) * ts
                nat_scr[pl.ds(base + r, rows, stride=dil), :] = ref[r, :, j * LANES:(j + 1) * LANES]
        natural.append(jnp.concatenate(
            [nat_scr[(n * lane_tiles + j) * ts:(n * lane_tiles + j + 1) * ts, :]
             for j in range(lane_tiles)], axis=1))
    o0, o1, o2, l0, l1, l2 = natural
    m = jnp.maximum(jnp.maximum(l0, l1), l2)
    e0, e1, e2 = jnp.exp(l0 - m), jnp.exp(l1 - m), jnp.exp(l2 - m)
    y = (e0 * o0 + e1 * o1 + e2 * o2) / (e0 + e1 + e2)
    y_attn = jnp.dot(y.astype(BF16), wap_ref[...], preferred_element_type=F32)

    keep_prev = jnp.where(i > 0, 1.0, 0.0)
    keep_next = jnp.where(i < pl.num_programs(1) - 1, 1.0, 0.0)
    ue_scr[:, 0:HALO, :] = up_ref[...] * keep_prev
    ue_scr[:, HALO:HALO + ts, :] = u_ref[...]
    ue_scr[:, HALO + ts:, :] = un_ref[...] * keep_next
    pad = conv_width // 2
    row_blocks = ts // CONV_ROWS

    def conv_block(n, carry):
        ct = n // row_blocks
        base = pl.multiple_of((n % row_blocks) * CONV_ROWS, CONV_ROWS)
        acc = jnp.broadcast_to(cb_ref[ct], (CONV_ROWS, LANES))
        for j in range(conv_width):
            acc = acc + ue_scr[ct, pl.ds(base + (HALO - pad + j), CONV_ROWS), :] * cw_ref[ct, j:j + 1, :]
        cv_scr[ct, pl.ds(base, CONV_ROWS), :] = acc
        return carry

    lax.fori_loop(0, n_ct * row_blocks, conv_block, 0)

    row_sum = jnp.zeros((ts, 1), F32)
    for ct in range(n_ct):
        row_sum = row_sum + jnp.sum(cv_scr[ct], axis=-1, keepdims=True)
    mu = row_sum * (1.0 / d)
    sq_sum = jnp.zeros((ts, 1), F32)
    for ct in range(n_ct):
        dev = cv_scr[ct] - mu
        sq_sum = sq_sum + jnp.sum(dev * dev, axis=-1, keepdims=True)
    rstd = lax.rsqrt(sq_sum * (1.0 / d) + EPS)
    acts = []
    for ct in range(n_ct):
        cols = slice(ct * LANES, (ct + 1) * LANES)
        ln = (cv_scr[ct] - mu) * rstd * lg_ref[:, cols] + lb_ref[:, cols]
        acts.append((ln * jax.nn.sigmoid(ln)).astype(BF16))
    y_conv = jnp.dot(jnp.concatenate(acts, axis=1), wcp_ref[...],
                     preferred_element_type=F32) + bcp_ref[...]

    n_tc = t_ref.shape[0] // 2
    g_attn = jnp.concatenate([t_ref[c] for c in range(n_tc)], axis=1)
    g_conv = jnp.concatenate([t_ref[n_tc + c] for c in range(n_tc)], axis=1)
    merged = (g_attn * y_attn + g_conv * y_conv).astype(BF16)
    out_ref[...] = x_ref[...] + jnp.dot(merged, wo_ref[...], preferred_element_type=F32)


def _mix_out(x, outs, lses, u, gates, w_attn_proj, conv_w, conv_b, ln_g, ln_b,
             w_conv_proj, b_conv_proj, w_out, batch, seq, ts):
    t, d = x.shape
    n_ct = u.shape[0]
    conv_width = conv_w.shape[0]
    dils = tuple(dil for _, dil in DILATION_GROUPS)
    tiles = seq // ts
    hb = ts // HALO
    n_hb = seq // HALO
    row = lambda b, i: (b * tiles + i, 0)
    ol_specs = [pl.BlockSpec((None, dil, ts // dil, GROUP_WIDTH), lambda b, i: (b, 0, i, 0))
                for dil in dils]
    u_spec = pl.BlockSpec((n_ct, ts, LANES), lambda b, i: (0, b * tiles + i, 0))
    up_spec = pl.BlockSpec((n_ct, HALO, LANES),
                           lambda b, i: (0, b * n_hb + jnp.maximum(i * hb - 1, 0), 0))
    un_spec = pl.BlockSpec((n_ct, HALO, LANES),
                           lambda b, i: (0, b * n_hb + jnp.minimum((i + 1) * hb, n_hb - 1), 0))
    t_spec = pl.BlockSpec((gates.shape[0], ts, CHUNK), lambda b, i: (0, b * tiles + i, 0))
    cw = jnp.pad(conv_w, ((0, -conv_width % 8), (0, 0)))
    cw = cw.reshape(cw.shape[0], n_ct, LANES).transpose(1, 0, 2)
    weights = [w_attn_proj.astype(BF16), cw, conv_b.reshape(n_ct, 1, LANES), ln_g.reshape(1, d),
               ln_b.reshape(1, d), w_conv_proj.astype(BF16), b_conv_proj.reshape(1, d),
               w_out.astype(BF16)]
    vmem = (2 * 6 * ts * GROUP_WIDTH * 4 + 2 * (ts + 2 * HALO) * d * 4 + 2 * 2 * ts * d * 4
            + 4 * ts * d * 4 + (GROUP_WIDTH + 2 * d) * d * 2 + 2 * (ts + 2 * HALO) * d * 4
            + 6 * ts * GROUP_WIDTH * 4 + 12 * ts * d * 4 + (8 << 20))
    return pl.pallas_call(
        functools.partial(_mix_body, conv_width, dils),
        out_shape=jax.ShapeDtypeStruct((t, d), F32),
        grid=(batch, tiles),
        in_specs=ol_specs + ol_specs + [u_spec, up_spec, un_spec, t_spec, pl.BlockSpec((ts, d), row)]
                 + [_const_spec(w.shape) for w in weights],
        out_specs=pl.BlockSpec((ts, d), row),
        scratch_shapes=[pltpu.VMEM((n_ct, ts + 2 * HALO, LANES), F32),
                        pltpu.VMEM((n_ct, ts, LANES), F32),
                        pltpu.VMEM((2 * N_GROUPS * (GROUP_WIDTH // LANES) * ts, LANES), F32)],
        compiler_params=_params(2, vmem),
        name="mix_out",
    )(*outs, *lses, u, u, u, gates, x, *weights)


def kernel(x, ffn1_norm, ffn1_w_up, ffn1_w_down, mix_norm, w_in, b_in, q_norm, k_norm,
           w_attn_proj, conv_w, conv_b, conv_ln_g, conv_ln_b, w_conv_proj, b_conv_proj,
           w_out, ffn2_norm, ffn2_w_up, ffn2_w_down, final_norm):
    batch, seq, d = x.shape
    depth = ffn1_norm.shape[0]
    proj_tm = 512
    rope = _rope_tables(seq, proj_tm, tuple(dil for _, dil in DILATION_GROUPS))
    h = x.reshape(batch * seq, d)
    for l in range(depth):
        h = _ffn(h, ffn1_norm[l], ffn1_w_up[l], ffn1_w_down[l], None, tm=1024)
        *qkv, u, gates = _proj(h, mix_norm[l], w_in[l], b_in[l], q_norm[l], k_norm[l],
                               rope, batch, seq, proj_tm)
        outs, lses = zip(*[_attn_group(*qkv[3 * g:3 * g + 3], g) for g in range(N_GROUPS)])
        h = _mix_out(h, outs, lses, u, gates, w_attn_proj[l], conv_w[l], conv_b[l],
                     conv_ln_g[l], conv_ln_b[l], w_conv_proj[l], b_conv_proj[l], w_out[l],
                     batch, seq, ts=256)
        h = _ffn(h, ffn2_norm[l], ffn2_w_up[l], ffn2_w_down[l], final_norm[l], tm=1024)
    return h.reshape(batch, seq, d)
```

```python
import functools
import math

import numpy as np

import jax
import jax.numpy as jnp
from jax import lax
from jax.experimental import pallas as pl
from jax.experimental.pallas import tpu as pltpu

HEAD_DIM = 64
HEADS_PER_GROUP = 8
DILATION_GROUPS = ((128, 1), (512, 4), (2048, 16))
DILATIONS = tuple(dil for _, dil in DILATION_GROUPS)
N_GROUPS = len(DILATION_GROUPS)
GROUP_WIDTH = HEADS_PER_GROUP * HEAD_DIM
ATTN_WIDTH = N_GROUPS * GROUP_WIDTH
ROPE_THETA = 10000.0
EPS = 1e-6
NEG = -1e30

LANES = 128
CHUNK = 256
VMEM_BYTES = 64 * 1024 * 1024
HALO = 16
Q_BLOCK = 128
CONV_ROWS = 128
HEADS_PER_CHUNK = CHUNK // HEAD_DIM
CHUNKS_PER_GROUP = GROUP_WIDTH // CHUNK
TILES_PER_CHUNK = CHUNK // LANES
HALF = HEAD_DIM // 2

F32 = jnp.float32
BF16 = jnp.bfloat16


def _rms(x, g):
    return x * lax.rsqrt(jnp.mean(x * x, axis=-1, keepdims=True) + EPS) * g


def _const_spec(shape):
    zeros = (0,) * len(shape)
    return pl.BlockSpec(shape, lambda *_: zeros, pipeline_mode=pl.Buffered(1))


def _params(n_axes, vmem_bytes):
    return pltpu.CompilerParams(
        dimension_semantics=("arbitrary",) * n_axes,
        vmem_limit_bytes=min(int(vmem_bytes), VMEM_BYTES - 4 * 1024 * 1024))


def _ffn_body(final_norm, x_ref, g_ref, wup_ref, wdn_ref, *rest):
    if final_norm:
        fg_ref, o_ref, h_scr, acc_scr = rest
    else:
        o_ref, h_scr, acc_scr = rest
    d_ff = wdn_ref.shape[0]
    h_scr[...] = _rms(x_ref[...], g_ref[...]).astype(BF16)
    for c in range(d_ff // CHUNK):
        h = h_scr[...]
        a = jnp.dot(h, wup_ref[:, c * CHUNK:(c + 1) * CHUNK], preferred_element_type=F32)
        b = jnp.dot(h, wup_ref[:, d_ff + c * CHUNK:d_ff + (c + 1) * CHUNK], preferred_element_type=F32)
        s = (a * jax.nn.sigmoid(a) * b).astype(BF16)
        y = jnp.dot(s, wdn_ref[c * CHUNK:(c + 1) * CHUNK, :], preferred_element_type=F32)
        if c == 0:
            acc_scr[...] = y
        else:
            acc_scr[...] += y
    y = x_ref[...] + 0.5 * acc_scr[...]
    if final_norm:
        y = _rms(y, fg_ref[...])
    o_ref[...] = y


def _ffn(x, norm_g, w_up, w_down, final_g, tm):
    t, d = x.shape
    d_ff = w_down.shape[0]
    final_norm = final_g is not None
    args = [x, norm_g.reshape(1, d), w_up.astype(BF16), w_down.astype(BF16)]
    specs = [pl.BlockSpec((tm, d), lambda i: (i, 0)), _const_spec((1, d)),
             _const_spec(w_up.shape), _const_spec(w_down.shape)]
    if final_norm:
        args.append(final_g.reshape(1, d))
        specs.append(_const_spec((1, d)))
    vmem = (4 * tm * d * 4 + 3 * d_ff * d * 2 + tm * d * 6 + 8 * tm * CHUNK * 4 + (8 << 20))
    return pl.pallas_call(
        functools.partial(_ffn_body, final_norm),
        out_shape=jax.ShapeDtypeStruct((t, d), F32),
        grid=(t // tm,),
        in_specs=specs,
        out_specs=pl.BlockSpec((tm, d), lambda i: (i, 0)),
        scratch_shapes=[pltpu.VMEM((tm, d), BF16), pltpu.VMEM((tm, d), F32)],
        compiler_params=_params(1, vmem),
        name="ffn_final" if final_norm else "ffn",
    )(*args)


def _rotary_cols(a):
    lead = a.shape[:-1]
    a = a.reshape(*lead, -1, HEADS_PER_CHUNK, TILES_PER_CHUNK, HALF)
    return jnp.swapaxes(a, -2, -3).reshape(*lead, -1)


def _split_w_in(w_in, b_in):
    def qkv(a):
        return jnp.concatenate([_rotary_cols(a[..., :ATTN_WIDTH]),
                                _rotary_cols(a[..., ATTN_WIDTH:2 * ATTN_WIDTH]),
                                a[..., 2 * ATTN_WIDTH:3 * ATTN_WIDTH]], axis=-1)
    b_in = b_in.reshape(1, -1)
    return ((qkv(w_in).astype(BF16), qkv(b_in)),
            (w_in[:, 3 * ATTN_WIDTH:].astype(BF16), b_in[:, 3 * ATTN_WIDTH:]))


def _head_gain(g):
    full = jnp.broadcast_to(g[:, None, :], (N_GROUPS, HEADS_PER_GROUP, HEAD_DIM))
    return _rotary_cols(full.reshape(1, ATTN_WIDTH))


def _rope_tables(seq, tm):
    inv = ROPE_THETA ** (-jnp.arange(0, HEAD_DIM, 2, dtype=F32) / HEAD_DIM)
    tables = []
    for dil in DILATIONS:
        row = np.arange(seq)
        tile, r, m = row // tm, (row % tm) // (tm // dil), row % (tm // dil)
        pos = jnp.asarray(tile * tm + m * dil + r, F32)
        ang = pos[:, None] * inv[None, :]
        cos = jnp.tile(jnp.cos(ang), (1, LANES // HALF))
        sin = jnp.tile(jnp.sin(ang), (1, LANES // HALF))
        scale = 1.0 / math.sqrt(HEAD_DIM)
        tables.append(jnp.stack([cos * scale, sin * scale, cos, sin]))
    return jnp.stack(tables)


def _segment_matrix():
    lane = np.arange(LANES)
    same = (lane[:, None] // HALF == lane[None, :] // HALF).astype(np.float32)
    return jnp.asarray(np.tile(same, (2, TILES_PER_CHUNK)), BF16)


def _head_norm_rope(z, seg_ref, gain, cos, sin):
    z0, z1 = z[:, :LANES], z[:, LANES:]
    w = z0 * z0 + z1 * z1
    w_hi = w.astype(BF16)
    w_lo = (w - w_hi.astype(F32)).astype(BF16)
    ss = jnp.dot(jnp.concatenate([w_hi, w_lo], axis=1), seg_ref[...], preferred_element_type=F32)
    t = z * lax.rsqrt(ss * (1.0 / HEAD_DIM) + EPS) * gain
    t0, t1 = t[:, :LANES], t[:, LANES:]
    return jnp.concatenate([t0 * cos - t1 * sin, t1 * cos + t0 * sin], axis=1)


def _qkv_body(x_ref, g_ref, w_ref, b_ref, gq_ref, gk_ref, rope_ref, seg_ref, *rest):
    qkv_refs = rest[:3 * N_GROUPS]
    hn_scr = rest[3 * N_GROUPS]
    h_scrs = rest[3 * N_GROUPS + 1:]
    tm = x_ref.shape[0]

    hn = _rms(x_ref[...], g_ref[...])
    h_scrs[0][...] = hn.astype(BF16)
    lane_tiles = hn_scr.shape[0] // tm
    for j in range(lane_tiles):
        hn_scr[j * tm:(j + 1) * tm, :] = hn[:, j * LANES:(j + 1) * LANES]
    for dil, h_scr in zip(DILATIONS[1:], h_scrs[1:]):
        rows = tm // dil
        for r in range(dil):
            for j in range(lane_tiles):
                src = hn_scr[pl.ds(j * tm + r, rows, stride=dil), :]
                h_scr[r * rows:(r + 1) * rows, j * LANES:(j + 1) * LANES] = src.astype(BF16)

    for g, dil in enumerate(DILATIONS):
        q_ref, k_ref, v_ref = qkv_refs[3 * g:3 * g + 3]
        h = h_scrs[g][...]
        shape = (dil, tm // dil, CHUNK)
        for c in range(CHUNKS_PER_GROUP):
            cols = slice((g * CHUNKS_PER_GROUP + c) * CHUNK, (g * CHUNKS_PER_GROUP + c + 1) * CHUNK)
            z = [jnp.dot(h, w_ref[:, n * ATTN_WIDTH + cols.start:n * ATTN_WIDTH + cols.stop],
                         preferred_element_type=F32)
                 + b_ref[:, n * ATTN_WIDTH + cols.start:n * ATTN_WIDTH + cols.stop] for n in range(3)]
            q = _head_norm_rope(z[0], seg_ref, gq_ref[:, cols], rope_ref[g, 0], rope_ref[g, 1])
            q_ref[c] = q.astype(BF16).reshape(shape)
            k = _head_norm_rope(z[1], seg_ref, gk_ref[:, cols], rope_ref[g, 2], rope_ref[g, 3])
            k_ref[c] = k.astype(BF16).reshape(shape)
            v_ref[c] = z[2].astype(BF16).reshape(shape)


def _qkv_proj(x, mix_g, w_qkv, b_qkv, q_norm, k_norm, rope, batch, seq, tm):
    t, d = x.shape
    gq, gk = _head_gain(q_norm), _head_gain(k_norm)
    seg = _segment_matrix()
    tiles = seq // tm
    consts = [w_qkv, b_qkv, gq, gk]
    args = [x, mix_g.reshape(1, d), *consts, rope, seg]
    specs = ([pl.BlockSpec((tm, d), lambda b, i: (b * tiles + i, 0)), _const_spec((1, d))]
             + [_const_spec(a.shape) for a in consts]
             + [pl.BlockSpec((N_GROUPS, 4, tm, LANES), lambda b, i: (0, 0, i, 0)),
                _const_spec(seg.shape)])
    out_shape, out_specs = [], []
    for dil in DILATIONS:
        shape = (CHUNKS_PER_GROUP, batch, dil, seq // dil, CHUNK)
        spec = pl.BlockSpec((CHUNKS_PER_GROUP, None, dil, tm // dil, CHUNK),
                            lambda b, i: (0, b, 0, i, 0))
        out_shape += [jax.ShapeDtypeStruct(shape, BF16)] * 3
        out_specs += [spec] * 3
    vmem = (2 * tm * d * 4 + 3 * ATTN_WIDTH * d * 2 + 2 * 3 * ATTN_WIDTH * tm * 2
            + tm * d * (4 + 2 * N_GROUPS) + 2 * N_GROUPS * 4 * tm * LANES * 4
            + 8 * tm * CHUNK * 4 + (6 << 20))
    return pl.pallas_call(
        _qkv_body,
        out_shape=out_shape,
        grid=(batch, tiles),
        in_specs=specs,
        out_specs=out_specs,
        scratch_shapes=[pltpu.VMEM((d // LANES * tm, LANES), F32)] + [pltpu.VMEM((tm, d), BF16)] * N_GROUPS,
        compiler_params=_params(2, vmem),
        name="qkv_proj",
    )(*args)


def _attn_body(sub_len, half_window, q_ref, k_ref, v_ref, o_ref, l_ref):
    qt_rows = q_ref.shape[1]
    key_rows = Q_BLOCK + 2 * half_window
    tile_start = pl.program_id(2) * qt_rows
    lane = lax.broadcasted_iota(jnp.int32, (1, CHUNK), 1)
    head_lanes = [((lane % LANES) // HALF == hh).astype(BF16) for hh in range(HEADS_PER_CHUNK)]
    v_head = lax.broadcasted_iota(jnp.int32, (Q_BLOCK, CHUNK), 1) // HEAD_DIM
    offset = (lax.broadcasted_iota(jnp.int32, (Q_BLOCK, key_rows), 0)
              - lax.broadcasted_iota(jnp.int32, (Q_BLOCK, key_rows), 1))

    def block(i, carry):
        r0 = pl.multiple_of(i * Q_BLOCK, Q_BLOCK)
        q0 = tile_start + r0
        start = pl.multiple_of(jnp.clip(q0 - half_window, 0, sub_len - key_rows), half_window)
        valid = jnp.abs(offset + (q0 - start)) <= half_window
        for c in range(CHUNKS_PER_GROUP):
            qb = q_ref[c, pl.ds(r0, Q_BLOCK), :]
            kw = k_ref[c, pl.ds(start, key_rows), :]
            vw = v_ref[c, pl.ds(start, key_rows), :]
            qs = jnp.concatenate([qb * head_lanes[hh] for hh in range(HEADS_PER_CHUNK)], axis=0)
            s = lax.dot_general(qs, kw, (((1,), (1,)), ((), ())), preferred_element_type=F32)
            ps, dens, lses = [], [], []
            for hh in range(HEADS_PER_CHUNK):
                sh = jnp.where(valid, s[hh * Q_BLOCK:(hh + 1) * Q_BLOCK], NEG)
                m = jnp.max(sh, axis=-1, keepdims=True)
                p = jnp.exp(sh - m)
                den = jnp.sum(p, axis=-1, keepdims=True)
                ps.append(p.astype(BF16))
                dens.append(den)
                lses.append(m + jnp.log(den))
            pv = jnp.dot(jnp.concatenate(ps, axis=0), vw, preferred_element_type=F32)
            o, dn, ls = pv[:Q_BLOCK], dens[0], lses[0]
            for hh in range(1, HEADS_PER_CHUNK):
                mine = v_head == hh
                o = jnp.where(mine, pv[hh * Q_BLOCK:(hh + 1) * Q_BLOCK], o)
                dn = jnp.where(mine, dens[hh], dn)
                ls = jnp.where(mine, lses[hh], ls)
            cols = slice(c * CHUNK, (c + 1) * CHUNK)
            o_ref[pl.ds(r0, Q_BLOCK), cols] = o / dn
            l_ref[pl.ds(r0, Q_BLOCK), cols] = jnp.broadcast_to(ls, (Q_BLOCK, CHUNK))
        return carry

    lax.fori_loop(0, qt_rows // Q_BLOCK, block, 0, unroll=2)


def _attn_group(q, k, v, g):
    window, dil = DILATION_GROUPS[g]
    half_window = window // (2 * dil)
    _, batch, _, sub_len, _ = q.shape
    qt = min(512, sub_len)
    q_spec = pl.BlockSpec((CHUNKS_PER_GROUP, None, None, qt, CHUNK), lambda b, r, t: (0, b, r, t, 0))
    kv_spec = pl.BlockSpec((CHUNKS_PER_GROUP, None, None, sub_len, CHUNK),
                           lambda b, r, t: (0, b, r, 0, 0))
    o_spec = pl.BlockSpec((None, None, qt, GROUP_WIDTH), lambda b, r, t: (b, r, t, 0))
    o_shape = jax.ShapeDtypeStruct((batch, dil, sub_len, GROUP_WIDTH), F32)
    vmem = (2 * CHUNKS_PER_GROUP * qt * CHUNK * 2 + 4 * CHUNKS_PER_GROUP * sub_len * CHUNK * 2
            + 4 * qt * GROUP_WIDTH * 4 + (8 << 20))
    return pl.pallas_call(
        functools.partial(_attn_body, sub_len, half_window),
        out_shape=[o_shape, o_shape],
        grid=(batch, dil, sub_len // qt),
        in_specs=[q_spec, kv_spec, kv_spec],
        out_specs=[o_spec, o_spec],
        compiler_params=_params(3, vmem),
        name=f"attn_d{dil}",
    )(q, k, v)


def _mix_body(conv_width, *refs):
    ol_refs = refs[:2 * N_GROUPS]
    (x_ref, xp_ref, xn_ref, g_ref, w_ref, b_ref,
     wap_ref, cw_ref, cb_ref, lg_ref, lb_ref, wcp_ref, bcp_ref, wo_ref,
     out_ref, h_scr, ue_scr, cv_scr, gate_scr, ya_scr, nat_scr) = refs[2 * N_GROUPS:]
    i = pl.program_id(1)
    ts, d = x_ref.shape
    ext = ts + 2 * HALO
    n_ct = d // LANES
    pad = conv_width // 2

    gain = g_ref[...]
    h_scr[0:HALO, :] = _rms(xp_ref[...], gain).astype(BF16)
    h_scr[HALO:HALO + ts, :] = _rms(x_ref[...], gain).astype(BF16)
    h_scr[HALO + ts:, :] = _rms(xn_ref[...], gain).astype(BF16)

    lane_tiles = GROUP_WIDTH // LANES
    natural = []
    for n, (ref, dil) in enumerate(zip(ol_refs, DILATIONS + DILATIONS)):
        if dil == 1:
            natural.append(ref[0])
            continue
        rows = ts // dil
        slot = (n % N_GROUPS - 1) * 2 + n // N_GROUPS
        for r in range(dil):
            for j in range(lane_tiles):
                base = (slot * lane_tiles + j) * ts
                nat_scr[pl.ds(base + r, rows, stride=dil), :] = ref[r, :, j * LANES:(j + 1) * LANES]
        natural.append(jnp.concatenate(
            [nat_scr[(slot * lane_tiles + j) * ts:(slot * lane_tiles + j + 1) * ts, :]
             for j in range(lane_tiles)], axis=1))
    o0, o1, o2, l0, l1, l2 = natural
    m = jnp.maximum(jnp.maximum(l0, l1), l2)
    e0, e1, e2 = jnp.exp(l0 - m), jnp.exp(l1 - m), jnp.exp(l2 - m)
    y = (e0 * o0 + e1 * o1 + e2 * o2) / (e0 + e1 + e2)
    ya_scr[...] = jnp.dot(y.astype(BF16), wap_ref[...], preferred_element_type=F32)

    row = lax.broadcasted_iota(jnp.int32, (ext, 1), 0)
    inside = (((row >= HALO) | (i > 0)) & ((row < HALO + ts) | (i < pl.num_programs(1) - 1)))
    h_all = h_scr[...]

    def proj(h, c):
        cols = slice(c * CHUNK, (c + 1) * CHUNK)
        return jnp.dot(h, w_ref[:, cols], preferred_element_type=F32) + b_ref[:, cols]

    glu_chunks = d // CHUNK
    for c in range(glu_chunks):
        u = jnp.where(inside, proj(h_all, c) * jax.nn.sigmoid(proj(h_all, glu_chunks + c)), 0.0)
        for j in range(TILES_PER_CHUNK):
            ct = c * TILES_PER_CHUNK + j
            ue_scr[ct * ext:(ct + 1) * ext, :] = u[:, j * LANES:(j + 1) * LANES]

    h_tile = h_scr[HALO:HALO + ts, :]
    for c in range(gate_scr.shape[0]):
        gate_scr[c] = jax.nn.sigmoid(proj(h_tile, 2 * glu_chunks + c))

    def conv_tile(ct, carry):
        src = ct * ext + (HALO - pad)
        dst = pl.multiple_of(ct * ts, CONV_ROWS)
        for rb in range(ts // CONV_ROWS):
            acc = jnp.broadcast_to(cb_ref[ct], (CONV_ROWS, LANES))
            for j in range(conv_width):
                taps = ue_scr[pl.ds(src + rb * CONV_ROWS + j, CONV_ROWS), :]
                acc = acc + taps * cw_ref[ct, j:j + 1, :]
            cv_scr[pl.ds(dst + rb * CONV_ROWS, CONV_ROWS), :] = acc
        return carry

    lax.fori_loop(0, n_ct, conv_tile, 0)

    conv = [cv_scr[ct * ts:(ct + 1) * ts, :] for ct in range(n_ct)]
    row_sum = jnp.zeros((ts, 1), F32)
    for ct in range(n_ct):
        row_sum = row_sum + jnp.sum(conv[ct], axis=-1, keepdims=True)
    mu = row_sum * (1.0 / d)
    sq_sum = jnp.zeros((ts, 1), F32)
    for ct in range(n_ct):
        dev = conv[ct] - mu
        sq_sum = sq_sum + jnp.sum(dev * dev, axis=-1, keepdims=True)
    rstd = lax.rsqrt(sq_sum * (1.0 / d) + EPS)
    acts = []
    for ct in range(n_ct):
        cols = slice(ct * LANES, (ct + 1) * LANES)
        ln = (conv[ct] - mu) * rstd * lg_ref[:, cols] + lb_ref[:, cols]
        acts.append((ln * jax.nn.sigmoid(ln)).astype(BF16))
    y_conv = jnp.dot(jnp.concatenate(acts, axis=1), wcp_ref[...],
                     preferred_element_type=F32) + bcp_ref[...]

    n_tc = gate_scr.shape[0] // 2
    g_attn = jnp.concatenate([gate_scr[c] for c in range(n_tc)], axis=1)
    g_conv = jnp.concatenate([gate_scr[n_tc + c] for c in range(n_tc)], axis=1)
    merged = (g_attn * ya_scr[...] + g_conv * y_conv).astype(BF16)
    out_ref[...] = x_ref[...] + jnp.dot(merged, wo_ref[...], preferred_element_type=F32)


def _mix_out(x, outs, lses, mix_g, w_tail, b_tail, w_attn_proj, conv_w, conv_b, ln_g, ln_b,
             w_conv_proj, b_conv_proj, w_out, batch, seq, ts):
    t, d = x.shape
    n_ct = d // LANES
    conv_width = conv_w.shape[0]
    tiles = seq // ts
    hb = ts // HALO
    n_hb = seq // HALO
    ext = ts + 2 * HALO
    row = lambda b, i: (b * tiles + i, 0)
    ol_specs = [pl.BlockSpec((None, dil, ts // dil, GROUP_WIDTH), lambda b, i: (b, 0, i, 0))
                for dil in DILATIONS]
    xp_spec = pl.BlockSpec((HALO, d), lambda b, i: (b * n_hb + jnp.maximum(i * hb - 1, 0), 0))
    xn_spec = pl.BlockSpec((HALO, d), lambda b, i: (b * n_hb + jnp.minimum((i + 1) * hb, n_hb - 1), 0))
    cw = jnp.pad(conv_w, ((0, -conv_width % 8), (0, 0)))
    cw = cw.reshape(cw.shape[0], n_ct, LANES).transpose(1, 0, 2)
    consts = [mix_g.reshape(1, d), w_tail, b_tail,
              w_attn_proj.astype(BF16), cw, conv_b.reshape(n_ct, 1, LANES), ln_g.reshape(1, d),
              ln_b.reshape(1, d), w_conv_proj.astype(BF16), b_conv_proj.reshape(1, d),
              w_out.astype(BF16)]
    const_bytes = sum(a.size * a.dtype.itemsize for a in consts)
    scratch = [pltpu.VMEM((ext, d), BF16),
               pltpu.VMEM((n_ct * ext, LANES), F32),
               pltpu.VMEM((n_ct * ts, LANES), F32),
               pltpu.VMEM((2 * d // CHUNK, ts, CHUNK), F32),
               pltpu.VMEM((ts, d), F32),
               pltpu.VMEM((4 * (GROUP_WIDTH // LANES) * ts, LANES), F32)]
    scratch_bytes = ext * d * 2 + (ext + ts) * d * 4 + 3 * ts * d * 4 + 4 * ts * GROUP_WIDTH * 4
    vmem = (2 * 2 * N_GROUPS * ts * GROUP_WIDTH * 4 + 4 * ts * d * 4 + const_bytes + scratch_bytes
            + 10 * ts * d * 4 + (4 << 20))
    return pl.pallas_call(
        functools.partial(_mix_body, conv_width),
        out_shape=jax.ShapeDtypeStruct((t, d), F32),
        grid=(batch, tiles),
        in_specs=ol_specs + ol_specs + [pl.BlockSpec((ts, d), row), xp_spec, xn_spec]
                 + [_const_spec(a.shape) for a in consts],
        out_specs=pl.BlockSpec((ts, d), row),
        scratch_shapes=scratch,
        compiler_params=_params(2, vmem),
        name="mix_out",
    )(*outs, *lses, x, x, x, *consts)


def kernel(x, ffn1_norm, ffn1_w_up, ffn1_w_down, mix_norm, w_in, b_in, q_norm, k_norm,
           w_attn_proj, conv_w, conv_b, conv_ln_g, conv_ln_b, w_conv_proj, b_conv_proj,
           w_out, ffn2_norm, ffn2_w_up, ffn2_w_down, final_norm):
    batch, seq, d = x.shape
    depth = ffn1_norm.shape[0]
    proj_tm = 512
    rope = _rope_tables(seq, proj_tm)
    h = x.reshape(batch * seq, d)
    for l in range(depth):
        h = _ffn(h, ffn1_norm[l], ffn1_w_up[l], ffn1_w_down[l], None, tm=1024)
        (w_qkv, b_qkv), (w_tail, b_tail) = _split_w_in(w_in[l], b_in[l])
        qkv = _qkv_proj(h, mix_norm[l], w_qkv, b_qkv, q_norm[l], k_norm[l], rope, batch, seq, proj_tm)
        outs, lses = zip(*[_attn_group(*qkv[3 * g:3 * g + 3], g) for g in range(N_GROUPS)])
        h = _mix_out(h, outs, lses, mix_norm[l], w_tail, b_tail, w_attn_proj[l], conv_w[l], conv_b[l],
                     conv_ln_g[l], conv_ln_b[l], w_conv_proj[l], b_conv_proj[l], w_out[l],
                     batch, seq, ts=512)
        h = _ffn(h, ffn2_norm[l], ffn2_w_up[l], ffn2_w_down[l], final_norm[l], tm=1024)
    return h.reshape(batch, seq, d)
```

```python
import functools
import math

import numpy as np

import jax
import jax.numpy as jnp
from jax import lax
from jax.experimental import pallas as pl
from jax.experimental.pallas import tpu as pltpu

HEAD_DIM = 64
HEADS_PER_GROUP = 8
DILATION_GROUPS = ((128, 1), (512, 4), (2048, 16))
DILATIONS = tuple(dil for _, dil in DILATION_GROUPS)
N_GROUPS = len(DILATION_GROUPS)
GROUP_WIDTH = HEADS_PER_GROUP * HEAD_DIM
ATTN_WIDTH = N_GROUPS * GROUP_WIDTH
ROPE_THETA = 10000.0
EPS = 1e-6
NEG = -1e30

LANES = 128
CHUNK = 256
VMEM_BYTES = 64 * 1024 * 1024
HALO = 16
Q_BLOCK = 128
CONV_ROWS = 128
HEADS_PER_CHUNK = CHUNK // HEAD_DIM
CHUNKS_PER_GROUP = GROUP_WIDTH // CHUNK
TILES_PER_CHUNK = CHUNK // LANES
HALF = HEAD_DIM // 2

F32 = jnp.float32
BF16 = jnp.bfloat16


def _rms(x, g):
    return x * lax.rsqrt(jnp.mean(x * x, axis=-1, keepdims=True) + EPS) * g


def _const_spec(shape):
    zeros = (0,) * len(shape)
    return pl.BlockSpec(shape, lambda *_: zeros, pipeline_mode=pl.Buffered(1))


def _params(n_axes, vmem_bytes):
    return pltpu.CompilerParams(
        dimension_semantics=("arbitrary",) * n_axes,
        vmem_limit_bytes=min(int(vmem_bytes), VMEM_BYTES - 4 * 1024 * 1024))


def _ffn_body(final_norm, x_ref, g_ref, wup_ref, wdn_ref, *rest):
    if final_norm:
        fg_ref, o_ref, h_scr, acc_scr = rest
    else:
        o_ref, h_scr, acc_scr = rest
    d_ff = wdn_ref.shape[0]
    h_scr[...] = _rms(x_ref[...], g_ref[...]).astype(BF16)
    for c in range(d_ff // CHUNK):
        h = h_scr[...]
        a = jnp.dot(h, wup_ref[:, c * CHUNK:(c + 1) * CHUNK], preferred_element_type=F32)
        b = jnp.dot(h, wup_ref[:, d_ff + c * CHUNK:d_ff + (c + 1) * CHUNK], preferred_element_type=F32)
        s = (a * jax.nn.sigmoid(a) * b).astype(BF16)
        y = jnp.dot(s, wdn_ref[c * CHUNK:(c + 1) * CHUNK, :], preferred_element_type=F32)
        if c == 0:
            acc_scr[...] = y
        else:
            acc_scr[...] += y
    y = x_ref[...] + 0.5 * acc_scr[...]
    if final_norm:
        y = _rms(y, fg_ref[...])
    o_ref[...] = y


def _ffn(x, norm_g, w_up, w_down, final_g, tm):
    t, d = x.shape
    d_ff = w_down.shape[0]
    final_norm = final_g is not None
    args = [x, norm_g.reshape(1, d), w_up.astype(BF16), w_down.astype(BF16)]
    specs = [pl.BlockSpec((tm, d), lambda i: (i, 0)), _const_spec((1, d)),
             _const_spec(w_up.shape), _const_spec(w_down.shape)]
    if final_norm:
        args.append(final_g.reshape(1, d))
        specs.append(_const_spec((1, d)))
    vmem = (4 * tm * d * 4 + 3 * d_ff * d * 2 + tm * d * 6 + 8 * tm * CHUNK * 4 + (8 << 20))
    return pl.pallas_call(
        functools.partial(_ffn_body, final_norm),
        out_shape=jax.ShapeDtypeStruct((t, d), F32),
        grid=(t // tm,),
        in_specs=specs,
        out_specs=pl.BlockSpec((tm, d), lambda i: (i, 0)),
        scratch_shapes=[pltpu.VMEM((tm, d), BF16), pltpu.VMEM((tm, d), F32)],
        compiler_params=_params(1, vmem),
        name="ffn_final" if final_norm else "ffn",
    )(*args)


def _rotary_cols(a):
    lead = a.shape[:-1]
    a = a.reshape(*lead, -1, HEADS_PER_CHUNK, TILES_PER_CHUNK, HALF)
    return jnp.swapaxes(a, -2, -3).reshape(*lead, -1)


def _split_w_in(w_in, b_in):
    def qkv(a):
        return jnp.concatenate([_rotary_cols(a[..., :ATTN_WIDTH]),
                                _rotary_cols(a[..., ATTN_WIDTH:2 * ATTN_WIDTH]),
                                a[..., 2 * ATTN_WIDTH:3 * ATTN_WIDTH]], axis=-1)
    b_in = b_in.reshape(1, -1)
    return ((qkv(w_in).astype(BF16), qkv(b_in)),
            (w_in[:, 3 * ATTN_WIDTH:].astype(BF16), b_in[:, 3 * ATTN_WIDTH:]))


def _head_gain(g):
    full = jnp.broadcast_to(g[:, None, :], (N_GROUPS, HEADS_PER_GROUP, HEAD_DIM))
    return _rotary_cols(full.reshape(1, ATTN_WIDTH))


def _rope_tables(seq, tm):
    inv = ROPE_THETA ** (-jnp.arange(0, HEAD_DIM, 2, dtype=F32) / HEAD_DIM)
    tables = []
    for dil in DILATIONS:
        row = np.arange(seq)
        tile, r, m = row // tm, (row % tm) // (tm // dil), row % (tm // dil)
        pos = jnp.asarray(tile * tm + m * dil + r, F32)
        ang = pos[:, None] * inv[None, :]
        cos = jnp.tile(jnp.cos(ang), (1, LANES // HALF))
        sin = jnp.tile(jnp.sin(ang), (1, LANES // HALF))
        scale = 1.0 / math.sqrt(HEAD_DIM)
        tables.append(jnp.stack([cos * scale, sin * scale, cos, sin]))
    return jnp.stack(tables)


def _segment_matrix():
    lane = np.arange(LANES)
    same = (lane[:, None] // HALF == lane[None, :] // HALF).astype(np.float32)
    return jnp.asarray(np.tile(same, (2, TILES_PER_CHUNK)), BF16)


def _head_norm_rope(z, seg_ref, gain, cos, sin):
    z0, z1 = z[:, :LANES], z[:, LANES:]
    w = z0 * z0 + z1 * z1
    w_hi = w.astype(BF16)
    w_lo = (w - w_hi.astype(F32)).astype(BF16)
    ss = jnp.dot(jnp.concatenate([w_hi, w_lo], axis=1), seg_ref[...], preferred_element_type=F32)
    t = z * lax.rsqrt(ss * (1.0 / HEAD_DIM) + EPS) * gain
    t0, t1 = t[:, :LANES], t[:, LANES:]
    return jnp.concatenate([t0 * cos - t1 * sin, t1 * cos + t0 * sin], axis=1)


def _qkv_body(x_ref, g_ref, w_ref, b_ref, gq_ref, gk_ref, rope_ref, seg_ref, *rest):
    qkv_refs = rest[:3 * N_GROUPS]
    h_ref, hn_scr = rest[3 * N_GROUPS:3 * N_GROUPS + 2]
    h_scrs = (h_ref,) + rest[3 * N_GROUPS + 2:]
    tm = x_ref.shape[0]

    hn = _rms(x_ref[...], g_ref[...])
    h_ref[...] = hn.astype(BF16)
    lane_tiles = hn_scr.shape[0] // tm
    for j in range(lane_tiles):
        hn_scr[j * tm:(j + 1) * tm, :] = hn[:, j * LANES:(j + 1) * LANES]
    for dil, h_scr in zip(DILATIONS[1:], h_scrs[1:]):
        rows = tm // dil
        for r in range(dil):
            for j in range(lane_tiles):
                src = hn_scr[pl.ds(j * tm + r, rows, stride=dil), :]
                h_scr[r * rows:(r + 1) * rows, j * LANES:(j + 1) * LANES] = src.astype(BF16)

    for g, dil in enumerate(DILATIONS):
        q_ref, k_ref, v_ref = qkv_refs[3 * g:3 * g + 3]
        h = h_scrs[g][...]
        shape = (dil, tm // dil, CHUNK)
        for c in range(CHUNKS_PER_GROUP):
            cols = slice((g * CHUNKS_PER_GROUP + c) * CHUNK, (g * CHUNKS_PER_GROUP + c + 1) * CHUNK)
            z = [jnp.dot(h, w_ref[:, n * ATTN_WIDTH + cols.start:n * ATTN_WIDTH + cols.stop],
                         preferred_element_type=F32)
                 + b_ref[:, n * ATTN_WIDTH + cols.start:n * ATTN_WIDTH + cols.stop] for n in range(3)]
            q = _head_norm_rope(z[0], seg_ref, gq_ref[:, cols], rope_ref[g, 0], rope_ref[g, 1])
            q_ref[c] = q.astype(BF16).reshape(shape)
            k = _head_norm_rope(z[1], seg_ref, gk_ref[:, cols], rope_ref[g, 2], rope_ref[g, 3])
            k_ref[c] = k.astype(BF16).reshape(shape)
            v_ref[c] = z[2].astype(BF16).reshape(shape)


def _qkv_proj(x, mix_g, w_qkv, b_qkv, q_norm, k_norm, rope, batch, seq, tm):
    t, d = x.shape
    gq, gk = _head_gain(q_norm), _head_gain(k_norm)
    seg = _segment_matrix()
    tiles = seq // tm
    consts = [w_qkv, b_qkv, gq, gk]
    args = [x, mix_g.reshape(1, d), *consts, rope, seg]
    specs = ([pl.BlockSpec((tm, d), lambda b, i: (b * tiles + i, 0)), _const_spec((1, d))]
             + [_const_spec(a.shape) for a in consts]
             + [pl.BlockSpec((N_GROUPS, 4, tm, LANES), lambda b, i: (0, 0, i, 0)),
                _const_spec(seg.shape)])
    out_shape, out_specs = [], []
    for dil in DILATIONS:
        shape = (CHUNKS_PER_GROUP, batch, dil, seq // dil, CHUNK)
        spec = pl.BlockSpec((CHUNKS_PER_GROUP, None, dil, tm // dil, CHUNK),
                            lambda b, i: (0, b, 0, i, 0))
        out_shape += [jax.ShapeDtypeStruct(shape, BF16)] * 3
        out_specs += [spec] * 3
    out_shape.append(jax.ShapeDtypeStruct((t, d), BF16))
    out_specs.append(pl.BlockSpec((tm, d), lambda b, i: (b * tiles + i, 0)))
    vmem =(2 * tm * d * 4 + 3 * ATTN_WIDTH * d * 2 + 2 * 3 * ATTN_WIDTH * tm * 2
            + tm * d * (4 + 2 * N_GROUPS) + 2 * N_GROUPS * 4 * tm * LANES * 4
            + 8 * tm * CHUNK * 4 + (6 << 20))
    return pl.pallas_call(
        _qkv_body,
        out_shape=out_shape,
        grid=(batch, tiles),
        in_specs=specs,
        out_specs=out_specs,
        scratch_shapes=[pltpu.VMEM((d // LANES * tm, LANES), F32)] + [pltpu.VMEM((tm, d), BF16)] * (N_GROUPS - 1),
        compiler_params=_params(2, vmem),
        name="qkv_proj",
    )(*args)


def _attn_body(sub_len, half_window, q_ref, k_ref, v_ref, o_ref, l_ref):
    qt_rows = q_ref.shape[1]
    key_rows = Q_BLOCK + 2 * half_window
    tile_start = pl.program_id(2) * qt_rows
    lane = lax.broadcasted_iota(jnp.int32, (1, CHUNK), 1)
    head_lanes = [((lane % LANES) // HALF == hh).astype(BF16) for hh in range(HEADS_PER_CHUNK)]
    v_head = lax.broadcasted_iota(jnp.int32, (Q_BLOCK, CHUNK), 1) // HEAD_DIM
    offset = (lax.broadcasted_iota(jnp.int32, (Q_BLOCK, key_rows), 0)
              - lax.broadcasted_iota(jnp.int32, (Q_BLOCK, key_rows), 1))

    def block(i, carry):
        r0 = pl.multiple_of(i * Q_BLOCK, Q_BLOCK)
        q0 = tile_start + r0
        start = pl.multiple_of(jnp.clip(q0 - half_window, 0, sub_len - key_rows), half_window)
        valid = jnp.abs(offset + (q0 - start)) <= half_window
        for c in range(CHUNKS_PER_GROUP):
            qb = q_ref[c, pl.ds(r0, Q_BLOCK), :]
            kw = k_ref[c, pl.ds(start, key_rows), :]
            vw = v_ref[c, pl.ds(start, key_rows), :]
            qs = jnp.concatenate([qb * head_lanes[hh] for hh in range(HEADS_PER_CHUNK)], axis=0)
            s = lax.dot_general(qs, kw, (((1,), (1,)), ((), ())), preferred_element_type=F32)
            ps, dens, lses = [], [], []
            for hh in range(HEADS_PER_CHUNK):
                sh = jnp.where(valid, s[hh * Q_BLOCK:(hh + 1) * Q_BLOCK], NEG)
                m = jnp.max(sh, axis=-1, keepdims=True)
                p = jnp.exp(sh - m)
                den = jnp.sum(p, axis=-1, keepdims=True)
                ps.append(p.astype(BF16))
                dens.append(den)
                lses.append(m + jnp.log(den))
            pv = jnp.dot(jnp.concatenate(ps, axis=0), vw, preferred_element_type=F32)
            o, dn, ls = pv[:Q_BLOCK], dens[0], lses[0]
            for hh in range(1, HEADS_PER_CHUNK):
                mine = v_head == hh
                o = jnp.where(mine, pv[hh * Q_BLOCK:(hh + 1) * Q_BLOCK], o)
                dn = jnp.where(mine, dens[hh], dn)
                ls = jnp.where(mine, lses[hh], ls)
            cols = slice(c * CHUNK, (c + 1) * CHUNK)
            o_ref[pl.ds(r0, Q_BLOCK), cols] = o / dn
            l_ref[pl.ds(r0, Q_BLOCK), cols] = jnp.broadcast_to(ls, (Q_BLOCK, CHUNK))
        return carry

    n_blocks = qt_rows // Q_BLOCK
    lax.fori_loop(0, n_blocks, block, 0, unroll=min(4, n_blocks))


def _attn_group(q, k, v, g):
    window, dil = DILATION_GROUPS[g]
    half_window = window // (2 * dil)
    _, batch, _, sub_len, _ = q.shape
    qt = min(512, sub_len)
    q_spec = pl.BlockSpec((CHUNKS_PER_GROUP, None, None, qt, CHUNK), lambda b, r, t: (0, b, r, t, 0))
    kv_spec = pl.BlockSpec((CHUNKS_PER_GROUP, None, None, sub_len, CHUNK),
                           lambda b, r, t: (0, b, r, 0, 0))
    o_spec = pl.BlockSpec((None, None, qt, GROUP_WIDTH), lambda b, r, t: (b, r, t, 0))
    o_shape = jax.ShapeDtypeStruct((batch, dil, sub_len, GROUP_WIDTH), F32)
    vmem = (2 * CHUNKS_PER_GROUP * qt * CHUNK * 2 + 4 * CHUNKS_PER_GROUP * sub_len * CHUNK * 2
            + 4 * qt * GROUP_WIDTH * 4 + (8 << 20))
    return pl.pallas_call(
        functools.partial(_attn_body, sub_len, half_window),
        out_shape=[o_shape, o_shape],
        grid=(batch, dil, sub_len // qt),
        in_specs=[q_spec, kv_spec, kv_spec],
        out_specs=[o_spec, o_spec],
        compiler_params=_params(3, vmem),
        name=f"attn_d{dil}",
    )(q, k, v)


def _mix_body(conv_width, *refs):
    ol_refs = refs[:2 * N_GROUPS]
    (x_ref, h_ref, hp_ref, hn_ref, w_ref, b_ref,
     wap_ref, cw_ref, cb_ref, lg_ref, lb_ref, wcp_ref, bcp_ref, wo_ref,
     out_ref, ue_scr, cv_scr, gate_scr, ya_scr, nat_scr) = refs[2 * N_GROUPS:]
    i = pl.program_id(1)
    ts, d = x_ref.shape
    ext = ts + 2 * HALO
    n_ct = d // LANES
    pad = conv_width // 2

    row = lax.broadcasted_iota(jnp.int32, (ext, 1), 0)
    inside = (((row >= HALO) | (i > 0)) & ((row < HALO + ts) | (i < pl.num_programs(1) - 1)))
    h_tile = h_ref[...]
    h_all = jnp.concatenate([hp_ref[...], h_tile, hn_ref[...]], axis=0)

    def proj(h, c):
        cols = slice(c * CHUNK, (c + 1) * CHUNK)
        return jnp.dot(h, w_ref[:, cols], preferred_element_type=F32) + b_ref[:, cols]

    glu_chunks = d // CHUNK
    for c in range(glu_chunks):
        u = jnp.where(inside, proj(h_all, c) * jax.nn.sigmoid(proj(h_all, glu_chunks + c)), 0.0)
        for j in range(TILES_PER_CHUNK):
            ct = c * TILES_PER_CHUNK + j
            ue_scr[ct * ext:(ct + 1) * ext, :] = u[:, j * LANES:(j + 1) * LANES]

    lane_tiles = GROUP_WIDTH // LANES
    natural = []
    for n, (ref, dil) in enumerate(zip(ol_refs, DILATIONS + DILATIONS)):
        if dil == 1:
            natural.append(ref[0])
            continue
        rows = ts // dil
        slot = (n % N_GROUPS - 1) * 2 + n // N_GROUPS
        for r in range(dil):
            for j in range(lane_tiles):
                base = (slot * lane_tiles + j) * ts
                nat_scr[pl.ds(base + r, rows, stride=dil), :] = ref[r, :, j * LANES:(j + 1) * LANES]
        natural.append(jnp.concatenate(
            [nat_scr[(slot * lane_tiles + j) * ts:(slot * lane_tiles + j + 1) * ts, :]
             for j in range(lane_tiles)], axis=1))
    o0, o1, o2, l0, l1, l2 = natural
    m = jnp.maximum(jnp.maximum(l0, l1), l2)
    e0, e1, e2 = jnp.exp(l0 - m), jnp.exp(l1 - m), jnp.exp(l2 - m)
    y = (e0 * o0 + e1 * o1 + e2 * o2) / (e0 + e1 + e2)
    ya_scr[...] = jnp.dot(y.astype(BF16), wap_ref[...], preferred_element_type=F32)

    def conv_tile(ct, carry):
        src = ct * ext + (HALO - pad)
        dst = pl.multiple_of(ct * ts, CONV_ROWS)
        for rb in range(ts // CONV_ROWS):
            acc = jnp.broadcast_to(cb_ref[ct], (CONV_ROWS, LANES))
            for j in range(conv_width):
                taps = ue_scr[pl.ds(src + rb * CONV_ROWS + j, CONV_ROWS), :]
                acc = acc + taps * cw_ref[ct, j:j + 1, :]
            cv_scr[pl.ds(dst + rb * CONV_ROWS, CONV_ROWS), :] = acc
        return carry

    lax.fori_loop(0, n_ct, conv_tile, 0)

    for c in range(gate_scr.shape[0]):
        gate_scr[c] = jax.nn.sigmoid(proj(h_tile, 2 * glu_chunks + c))

    conv = [cv_scr[ct * ts:(ct + 1) * ts, :] for ct in range(n_ct)]
    row_sum = jnp.zeros((ts, 1), F32)
    for ct in range(n_ct):
        row_sum = row_sum + jnp.sum(conv[ct], axis=-1, keepdims=True)
    mu = row_sum * (1.0 / d)
    sq_sum = jnp.zeros((ts, 1), F32)
    for ct in range(n_ct):
        dev = conv[ct] - mu
        sq_sum = sq_sum + jnp.sum(dev * dev, axis=-1, keepdims=True)
    rstd = lax.rsqrt(sq_sum * (1.0 / d) + EPS)
    acts = []
    for ct in range(n_ct):
        cols = slice(ct * LANES, (ct + 1) * LANES)
        ln = (conv[ct] - mu) * rstd * lg_ref[:, cols] + lb_ref[:, cols]
        acts.append((ln * jax.nn.sigmoid(ln)).astype(BF16))
    y_conv = jnp.dot(jnp.concatenate(acts, axis=1), wcp_ref[...],
                     preferred_element_type=F32) + bcp_ref[...]

    n_tc = gate_scr.shape[0] // 2
    g_attn = jnp.concatenate([gate_scr[c] for c in range(n_tc)], axis=1)
    g_conv = jnp.concatenate([gate_scr[n_tc + c] for c in range(n_tc)], axis=1)
    merged = (g_attn * ya_scr[...] + g_conv * y_conv).astype(BF16)
    out_ref[...] = x_ref[...] + jnp.dot(merged, wo_ref[...], preferred_element_type=F32)


def _mix_out(x, h, outs, lses, w_tail, b_tail, w_attn_proj, conv_w, conv_b, ln_g, ln_b,
             w_conv_proj, b_conv_proj, w_out, batch, seq, ts):
    t, d = x.shape
    n_ct = d // LANES
    conv_width = conv_w.shape[0]
    tiles = seq // ts
    hb = ts // HALO
    n_hb = seq // HALO
    ext = ts + 2 * HALO
    row = lambda b, i: (b * tiles + i, 0)
    ol_specs = [pl.BlockSpec((None, dil, ts // dil, GROUP_WIDTH), lambda b, i: (b, 0, i, 0))
                for dil in DILATIONS]
    hp_spec = pl.BlockSpec((HALO, d), lambda b, i: (b * n_hb + jnp.maximum(i * hb - 1, 0), 0))
    hn_spec = pl.BlockSpec((HALO, d), lambda b, i: (b * n_hb + jnp.minimum((i + 1) * hb, n_hb - 1), 0))
    cw = jnp.pad(conv_w, ((0, -conv_width % 8), (0, 0)))
    cw = cw.reshape(cw.shape[0], n_ct, LANES).transpose(1, 0, 2)
    consts = [w_tail, b_tail,
              w_attn_proj.astype(BF16), cw, conv_b.reshape(n_ct, 1, LANES), ln_g.reshape(1, d),
              ln_b.reshape(1, d), w_conv_proj.astype(BF16), b_conv_proj.reshape(1, d),
              w_out.astype(BF16)]
    const_bytes = sum(a.size * a.dtype.itemsize for a in consts)
    scratch = [pltpu.VMEM((n_ct * ext, LANES), F32),
               pltpu.VMEM((n_ct * ts, LANES), F32),
               pltpu.VMEM((2 * d // CHUNK, ts, CHUNK), F32),
               pltpu.VMEM((ts, d), F32),
               pltpu.VMEM((4 * (GROUP_WIDTH // LANES) * ts, LANES), F32)]
    scratch_bytes = (ext + ts) * d * 4 + 3 * ts * d * 4 + 4 * ts * GROUP_WIDTH * 4
    vmem = (2 * 2 * N_GROUPS * ts * GROUP_WIDTH * 4 + 4 * ts * d * 4 + 2 * ext * d * 2
            + const_bytes + scratch_bytes
            + 10 * ts * d * 4 + (4 << 20))
    return pl.pallas_call(
        functools.partial(_mix_body, conv_width),
        out_shape=jax.ShapeDtypeStruct((t, d), F32),
        grid=(batch, tiles),
        in_specs=ol_specs + ol_specs + [pl.BlockSpec((ts, d), row)] * 2 + [hp_spec, hn_spec]
                 + [_const_spec(a.shape) for a in consts],
        out_specs=pl.BlockSpec((ts, d), row),
        scratch_shapes=scratch,
        compiler_params=_params(2, vmem),
        name="mix_out",
    )(*outs, *lses, x, h, h, h, *consts)


def kernel(x, ffn1_norm, ffn1_w_up, ffn1_w_down, mix_norm, w_in, b_in, q_norm, k_norm,
           w_attn_proj, conv_w, conv_b, conv_ln_g, conv_ln_b, w_conv_proj, b_conv_proj,
           w_out, ffn2_norm, ffn2_w_up, ffn2_w_down, final_norm):
    batch, seq, d = x.shape
    depth = ffn1_norm.shape[0]
    proj_tm = 512
    rope = _rope_tables(seq, proj_tm)
    h = x.reshape(batch * seq, d)
    for l in range(depth):
        h = _ffn(h, ffn1_norm[l], ffn1_w_up[l], ffn1_w_down[l], None, tm=1024)
        (w_qkv, b_qkv), (w_tail, b_tail) = _split_w_in(w_in[l], b_in[l])
        *qkv, hn = _qkv_proj(h, mix_norm[l], w_qkv, b_qkv, q_norm[l], k_norm[l], rope, batch, seq, proj_tm)
        outs, lses = zip(*[_attn_group(*qkv[3 * g:3 * g + 3], g) for g in range(N_GROUPS)])
        h = _mix_out(h, hn, outs, lses, w_tail, b_tail, w_attn_proj[l], conv_w[l], conv_b[l],
                     conv_ln_g[l], conv_ln_b[l], w_conv_proj[l], b_conv_proj[l], w_out[l],
                     batch, seq, ts=512)
        h = _ffn(h, ffn2_norm[l], ffn2_w_up[l], ffn2_w_down[l], final_norm[l], tm=1024)
    return h.reshape(batch, seq, d)
```

```python
import functools
import math

import numpy as np

import jax
import jax.numpy as jnp
from jax import lax
from jax.experimental import pallas as pl
from jax.experimental.pallas import tpu as pltpu

HEAD_DIM = 64
HEADS_PER_GROUP = 8
DILATION_GROUPS = ((128, 1), (512, 4), (2048, 16))
DILATIONS = tuple(dil for _, dil in DILATION_GROUPS)
N_GROUPS = len(DILATION_GROUPS)
GROUP_WIDTH = HEADS_PER_GROUP * HEAD_DIM
ATTN_WIDTH = N_GROUPS * GROUP_WIDTH
ROPE_THETA = 10000.0
EPS = 1e-6
NEG = -1e30

LANES = 128
CHUNK = 256
VMEM_BYTES = 64 * 1024 * 1024
HALO = 16
Q_BLOCK = 128
CONV_ROWS = 128
HEADS_PER_CHUNK = CHUNK // HEAD_DIM
CHUNKS_PER_GROUP = GROUP_WIDTH // CHUNK
TILES_PER_CHUNK = CHUNK // LANES
HALF = HEAD_DIM // 2

F32 = jnp.float32
BF16 = jnp.bfloat16


def _rms(x, g):
    return x * lax.rsqrt(jnp.mean(x * x, axis=-1, keepdims=True) + EPS) * g


def _const_spec(shape):
    zeros = (0,) * len(shape)
    return pl.BlockSpec(shape, lambda *_: zeros, pipeline_mode=pl.Buffered(1))


def _params(n_axes, vmem_bytes):
    return pltpu.CompilerParams(
        dimension_semantics=("arbitrary",) * n_axes,
        vmem_limit_bytes=min(int(vmem_bytes), VMEM_BYTES - 4 * 1024 * 1024))


def _ffn_body(final_norm, x_ref, g_ref, wup_ref, wdn_ref, *rest):
    if final_norm:
        fg_ref, o_ref, h_scr, acc_scr = rest
    else:
        o_ref, h_scr, acc_scr = rest
    d_ff = wdn_ref.shape[0]
    h_scr[...] = _rms(x_ref[...], g_ref[...]).astype(BF16)
    for c in range(d_ff // CHUNK):
        h = h_scr[...]
        a = jnp.dot(h, wup_ref[:, c * CHUNK:(c + 1) * CHUNK], preferred_element_type=F32)
        b = jnp.dot(h, wup_ref[:, d_ff + c * CHUNK:d_ff + (c + 1) * CHUNK], preferred_element_type=F32)
        s = (a * jax.nn.sigmoid(a) * b).astype(BF16)
        y = jnp.dot(s, wdn_ref[c * CHUNK:(c + 1) * CHUNK, :], preferred_element_type=F32)
        if c == 0:
            acc_scr[...] = y
        else:
            acc_scr[...] += y
    y = x_ref[...] + 0.5 * acc_scr[...]
    if final_norm:
        y = _rms(y, fg_ref[...])
    o_ref[...] = y


def _ffn(x, norm_g, w_up, w_down, final_g, tm):
    t, d = x.shape
    d_ff = w_down.shape[0]
    final_norm = final_g is not None
    args = [x, norm_g.reshape(1, d), w_up.astype(BF16), w_down.astype(BF16)]
    specs = [pl.BlockSpec((tm, d), lambda i: (i, 0)), _const_spec((1, d)),
             _const_spec(w_up.shape), _const_spec(w_down.shape)]
    if final_norm:
        args.append(final_g.reshape(1, d))
        specs.append(_const_spec((1, d)))
    vmem = (4 * tm * d * 4 + 3 * d_ff * d * 2 + tm * d * 6 + 8 * tm * CHUNK * 4 + (8 << 20))
    return pl.pallas_call(
        functools.partial(_ffn_body, final_norm),
        out_shape=jax.ShapeDtypeStruct((t, d), F32),
        grid=(t // tm,),
        in_specs=specs,
        out_specs=pl.BlockSpec((tm, d), lambda i: (i, 0)),
        scratch_shapes=[pltpu.VMEM((tm, d), BF16), pltpu.VMEM((tm, d), F32)],
        compiler_params=_params(1, vmem),
        name="ffn_final" if final_norm else "ffn",
    )(*args)


def _rotary_cols(a):
    lead = a.shape[:-1]
    a = a.reshape(*lead, -1, HEADS_PER_CHUNK, TILES_PER_CHUNK, HALF)
    return jnp.swapaxes(a, -2, -3).reshape(*lead, -1)


def _split_w_in(w_in, b_in):
    def qkv(a):
        return jnp.concatenate([_rotary_cols(a[..., :ATTN_WIDTH]),
                                _rotary_cols(a[..., ATTN_WIDTH:2 * ATTN_WIDTH]),
                                a[..., 2 * ATTN_WIDTH:3 * ATTN_WIDTH]], axis=-1)
    b_in = b_in.reshape(1, -1)
    return ((qkv(w_in).astype(BF16), qkv(b_in)),
            (w_in[:, 3 * ATTN_WIDTH:].astype(BF16), b_in[:, 3 * ATTN_WIDTH:]))


def _head_gain(g):
    full = jnp.broadcast_to(g[:, None, :], (N_GROUPS, HEADS_PER_GROUP, HEAD_DIM))
    return _rotary_cols(full.reshape(1, ATTN_WIDTH))


def _rope_tables(seq, tm):
    inv = ROPE_THETA ** (-jnp.arange(0, HEAD_DIM, 2, dtype=F32) / HEAD_DIM)
    tables = []
    for dil in DILATIONS:
        row = np.arange(seq)
        tile, r, m = row // tm, (row % tm) // (tm // dil), row % (tm // dil)
        pos = jnp.asarray(tile * tm + m * dil + r, F32)
        ang = pos[:, None] * inv[None, :]
        cos = jnp.tile(jnp.cos(ang), (1, LANES // HALF))
        sin = jnp.tile(jnp.sin(ang), (1, LANES // HALF))
        scale = 1.0 / math.sqrt(HEAD_DIM)
        tables.append(jnp.stack([cos * scale, sin * scale, cos, sin]))
    return jnp.stack(tables)


def _segment_matrix():
    lane = np.arange(LANES)
    same = (lane[:, None] // HALF == lane[None, :] // HALF).astype(np.float32)
    return jnp.asarray(np.tile(same, (2, TILES_PER_CHUNK)), BF16)


def _head_norm_rope(z, seg_ref, gain, cos, sin):
    z0, z1 = z[:, :LANES], z[:, LANES:]
    w = z0 * z0 + z1 * z1
    w_hi = w.astype(BF16)
    w_lo = (w - w_hi.astype(F32)).astype(BF16)
    ss = jnp.dot(jnp.concatenate([w_hi, w_lo], axis=1), seg_ref[...], preferred_element_type=F32)
    t = z * lax.rsqrt(ss * (1.0 / HEAD_DIM) + EPS) * gain
    t0, t1 = t[:, :LANES], t[:, LANES:]
    return jnp.concatenate([t0 * cos - t1 * sin, t1 * cos + t0 * sin], axis=1)


def _qkv_body(x_ref, g_ref, w_ref, b_ref, gq_ref, gk_ref, rope_ref, seg_ref, *rest):
    qkv_refs = rest[:3 * N_GROUPS]
    h_ref, hn_scr = rest[3 * N_GROUPS:3 * N_GROUPS + 2]
    h_scrs = (h_ref,) + rest[3 * N_GROUPS + 2:]
    tm = x_ref.shape[0]

    hn = _rms(x_ref[...], g_ref[...])
    h_ref[...] = hn.astype(BF16)
    lane_tiles = hn_scr.shape[0] // tm
    for j in range(lane_tiles):
        hn_scr[j * tm:(j + 1) * tm, :] = hn[:, j * LANES:(j + 1) * LANES]
    for dil, h_scr in zip(DILATIONS[1:], h_scrs[1:]):
        rows = tm // dil
        for r in range(dil):
            for j in range(lane_tiles):
                src = hn_scr[pl.ds(j * tm + r, rows, stride=dil), :]
                h_scr[r * rows:(r + 1) * rows, j * LANES:(j + 1) * LANES] = src.astype(BF16)

    for g, dil in enumerate(DILATIONS):
        q_ref, k_ref, v_ref = qkv_refs[3 * g:3 * g + 3]
        h = h_scrs[g][...]
        shape = (dil, tm // dil, CHUNK)
        for c in range(CHUNKS_PER_GROUP):
            cols = slice((g * CHUNKS_PER_GROUP + c) * CHUNK, (g * CHUNKS_PER_GROUP + c + 1) * CHUNK)
            z = [jnp.dot(h, w_ref[:, n * ATTN_WIDTH + cols.start:n * ATTN_WIDTH + cols.stop],
                         preferred_element_type=F32)
                 + b_ref[:, n * ATTN_WIDTH + cols.start:n * ATTN_WIDTH + cols.stop] for n in range(3)]
            q = _head_norm_rope(z[0], seg_ref, gq_ref[:, cols], rope_ref[g, 0], rope_ref[g, 1])
            q_ref[c] = q.astype(BF16).reshape(shape)
            k = _head_norm_rope(z[1], seg_ref, gk_ref[:, cols], rope_ref[g, 2], rope_ref[g, 3])
            k_ref[c] = k.astype(BF16).reshape(shape)
            v_ref[c] = z[2].astype(BF16).reshape(shape)


def _qkv_proj(x, mix_g, w_qkv, b_qkv, q_norm, k_norm, rope, batch, seq, tm):
    t, d = x.shape
    gq, gk = _head_gain(q_norm), _head_gain(k_norm)
    seg = _segment_matrix()
    tiles = seq // tm
    consts = [w_qkv, b_qkv, gq, gk]
    args = [x, mix_g.reshape(1, d), *consts, rope, seg]
    specs = ([pl.BlockSpec((tm, d), lambda i, b: (b * tiles + i, 0)), _const_spec((1, d))]
             + [_const_spec(a.shape) for a in consts]
             + [pl.BlockSpec((N_GROUPS, 4, tm, LANES), lambda i, b: (0, 0, i, 0)),
                _const_spec(seg.shape)])
    out_shape, out_specs = [], []
    for dil in DILATIONS:
        shape = (CHUNKS_PER_GROUP, batch, dil, seq // dil, CHUNK)
        spec = pl.BlockSpec((CHUNKS_PER_GROUP, None, dil, tm // dil, CHUNK),
                            lambda i, b: (0, b, 0, i, 0))
        out_shape += [jax.ShapeDtypeStruct(shape, BF16)] * 3
        out_specs += [spec] * 3
    out_shape.append(jax.ShapeDtypeStruct((t, d), BF16))
    out_specs.append(pl.BlockSpec((tm, d), lambda i, b: (b * tiles + i, 0)))
    vmem = (2 * tm * d * 4 + 3 * ATTN_WIDTH * d * 2 + 2 * 3 * ATTN_WIDTH * tm * 2
            + tm * d * (4 + 2 * N_GROUPS) + 2 * N_GROUPS * 4 * tm * LANES * 4
            + 8 * tm * CHUNK * 4 + (6 << 20))
    return pl.pallas_call(
        _qkv_body,
        out_shape=out_shape,
        grid=(tiles, batch),
        in_specs=specs,
        out_specs=out_specs,
        scratch_shapes=[pltpu.VMEM((d // LANES * tm, LANES), F32)] + [pltpu.VMEM((tm, d), BF16)] * (N_GROUPS - 1),
        compiler_params=_params(2, vmem),
        name="qkv_proj",
    )(*args)


def _attn_body(sub_len, half_window, q_ref, k_ref, v_ref, o_ref, l_ref):
    qt_rows = q_ref.shape[1]
    key_rows = Q_BLOCK + 2 * half_window
    tile_start = pl.program_id(2) * qt_rows
    lane = lax.broadcasted_iota(jnp.int32, (1, CHUNK), 1)
    head_lanes = [((lane % LANES) // HALF == hh).astype(BF16) for hh in range(HEADS_PER_CHUNK)]
    v_head = lax.broadcasted_iota(jnp.int32, (Q_BLOCK, CHUNK), 1) // HEAD_DIM
    offset = (lax.broadcasted_iota(jnp.int32, (Q_BLOCK, key_rows), 0)
              - lax.broadcasted_iota(jnp.int32, (Q_BLOCK, key_rows), 1))

    def block(i, carry):
        r0 = pl.multiple_of(i * Q_BLOCK, Q_BLOCK)
        q0 = tile_start + r0
        start = pl.multiple_of(jnp.clip(q0 - half_window, 0, sub_len - key_rows), half_window)
        band = jnp.where(jnp.abs(offset + (q0 - start)) <= half_window, 0.0, NEG)
        for c in range(CHUNKS_PER_GROUP):
            qb = q_ref[c, pl.ds(r0, Q_BLOCK), :]
            kw = k_ref[c, pl.ds(start, key_rows), :]
            vw = v_ref[c, pl.ds(start, key_rows), :]
            qs = jnp.concatenate([qb * head_lanes[hh] for hh in range(HEADS_PER_CHUNK)], axis=0)
            s = lax.dot_general(qs, kw, (((1,), (1,)), ((), ())), preferred_element_type=F32)
            ps, maxes, dens = [], [], []
            for hh in range(HEADS_PER_CHUNK):
                sh = s[hh * Q_BLOCK:(hh + 1) * Q_BLOCK] + band
                m = jnp.max(sh, axis=-1, keepdims=True)
                p = jnp.exp(sh - m)
                ps.append(p.astype(BF16))
                maxes.append(m)
                dens.append(jnp.sum(p, axis=-1, keepdims=True))
            pv = jnp.dot(jnp.concatenate(ps, axis=0), vw, preferred_element_type=F32)
            o, mx, dn = pv[:Q_BLOCK], maxes[0], dens[0]
            for hh in range(1, HEADS_PER_CHUNK):
                mine = v_head == hh
                o = jnp.where(mine, pv[hh * Q_BLOCK:(hh + 1) * Q_BLOCK], o)
                mx = jnp.where(mine, maxes[hh], mx)
                dn = jnp.where(mine, dens[hh], dn)
            cols = slice(c * CHUNK, (c + 1) * CHUNK)
            o_ref[pl.ds(r0, Q_BLOCK), cols] = o / dn
            l_ref[pl.ds(r0, Q_BLOCK), cols] = mx + jnp.log(dn)
        return carry

    n_blocks = qt_rows // Q_BLOCK
    lax.fori_loop(0, n_blocks, block, 0, unroll=min(4, n_blocks))


def _attn_group(q, k, v, g):
    window, dil = DILATION_GROUPS[g]
    half_window = window // (2 * dil)
    _, batch, _, sub_len, _ = q.shape
    qt = min(512, sub_len)
    q_spec = pl.BlockSpec((CHUNKS_PER_GROUP, None, None, qt, CHUNK), lambda b, r, t: (0, b, r, t, 0))
    kv_spec = pl.BlockSpec((CHUNKS_PER_GROUP, None, None, sub_len, CHUNK),
                           lambda b, r, t: (0, b, r, 0, 0))
    o_spec = pl.BlockSpec((None, None, qt, GROUP_WIDTH), lambda b, r, t: (b, r, t, 0))
    o_shape = jax.ShapeDtypeStruct((batch, dil, sub_len, GROUP_WIDTH), F32)
    vmem = (2 * CHUNKS_PER_GROUP * qt * CHUNK * 2 + 4 * CHUNKS_PER_GROUP * sub_len * CHUNK * 2
            + 4 * qt * GROUP_WIDTH * 4 + (8 << 20))
    return pl.pallas_call(
        functools.partial(_attn_body, sub_len, half_window),
        out_shape=[o_shape, o_shape],
        grid=(batch, dil, sub_len // qt),
        in_specs=[q_spec, kv_spec, kv_spec],
        out_specs=[o_spec, o_spec],
        compiler_params=_params(3, vmem),
        name=f"attn_d{dil}",
    )(q, k, v)


def _mix_body(conv_width, *refs):
    ol_refs = refs[:2 * N_GROUPS]
    (x_ref, h_ref, hp_ref, hn_ref, w_ref, b_ref,
     wap_ref, cw_ref, cb_ref, lg_ref, lb_ref, wcp_ref, bcp_ref, wo_ref,
     out_ref, ue_scr, cv_scr, gate_scr, ya_scr, nat_scr) = refs[2 * N_GROUPS:]
    i = pl.program_id(1)
    ts, d = x_ref.shape
    ext = ts + 2 * HALO
    n_ct = d // LANES
    pad = conv_width // 2

    row = lax.broadcasted_iota(jnp.int32, (ext, 1), 0)
    inside = (((row >= HALO) | (i > 0)) & ((row < HALO + ts) | (i < pl.num_programs(1) - 1)))
    h_tile = h_ref[...]
    h_all = jnp.concatenate([hp_ref[...], h_tile, hn_ref[...]], axis=0)

    def proj(h, c):
        cols = slice(c * CHUNK, (c + 1) * CHUNK)
        return jnp.dot(h, w_ref[:, cols], preferred_element_type=F32) + b_ref[:, cols]

    glu_chunks = d // CHUNK
    for c in range(glu_chunks):
        u = jnp.where(inside, proj(h_all, c) * jax.nn.sigmoid(proj(h_all, glu_chunks + c)), 0.0)
        for j in range(TILES_PER_CHUNK):
            ct = c * TILES_PER_CHUNK + j
            ue_scr[ct * ext:(ct + 1) * ext, :] = u[:, j * LANES:(j + 1) * LANES]

    lane_tiles = GROUP_WIDTH // LANES
    natural = []
    for n, (ref, dil) in enumerate(zip(ol_refs, DILATIONS + DILATIONS)):
        if dil == 1:
            natural.append(ref[0])
            continue
        rows = ts // dil
        slot = (n % N_GROUPS - 1) * 2 + n // N_GROUPS
        for r in range(dil):
            for j in range(lane_tiles):
                base = (slot * lane_tiles + j) * ts
                nat_scr[pl.ds(base + r, rows, stride=dil), :] = ref[r, :, j * LANES:(j + 1) * LANES]
        natural.append(jnp.concatenate(
            [nat_scr[(slot * lane_tiles + j) * ts:(slot * lane_tiles + j + 1) * ts, :]
             for j in range(lane_tiles)], axis=1))
    o0, o1, o2, l0, l1, l2 = natural
    m = jnp.maximum(jnp.maximum(l0, l1), l2)
    e0, e1, e2 = jnp.exp(l0 - m), jnp.exp(l1 - m), jnp.exp(l2 - m)
    y = (e0 * o0 + e1 * o1 + e2 * o2) / (e0 + e1 + e2)
    ya_scr[...] = jnp.dot(y.astype(BF16), wap_ref[...], preferred_element_type=F32)

    def conv_tile(ct, carry):
        src = ct * ext + (HALO - pad)
        dst = pl.multiple_of(ct * ts, CONV_ROWS)
        for rb in range(ts // CONV_ROWS):
            acc = jnp.broadcast_to(cb_ref[ct], (CONV_ROWS, LANES))
            for j in range(conv_width):
                taps = ue_scr[pl.ds(src + rb * CONV_ROWS + j, CONV_ROWS), :]
                acc = acc + taps * cw_ref[ct, j:j + 1, :]
            cv_scr[pl.ds(dst + rb * CONV_ROWS, CONV_ROWS), :] = acc
        return carry

    lax.fori_loop(0, n_ct, conv_tile, 0)

    for c in range(gate_scr.shape[0]):
        gate_scr[c] = jax.nn.sigmoid(proj(h_tile, 2 * glu_chunks + c))

    conv = [cv_scr[ct * ts:(ct + 1) * ts, :] for ct in range(n_ct)]
    row_sum = jnp.zeros((ts, 1), F32)
    for ct in range(n_ct):
        row_sum = row_sum + jnp.sum(conv[ct], axis=-1, keepdims=True)
    mu = row_sum * (1.0 / d)
    sq_sum = jnp.zeros((ts, 1), F32)
    for ct in range(n_ct):
        dev = conv[ct] - mu
        sq_sum = sq_sum + jnp.sum(dev * dev, axis=-1, keepdims=True)
    rstd = lax.rsqrt(sq_sum * (1.0 / d) + EPS)
    acts = []
    for ct in range(n_ct):
        cols = slice(ct * LANES, (ct + 1) * LANES)
        ln = (conv[ct] - mu) * rstd * lg_ref[:, cols] + lb_ref[:, cols]
        acts.append((ln * jax.nn.sigmoid(ln)).astype(BF16))
    y_conv = jnp.dot(jnp.concatenate(acts, axis=1), wcp_ref[...],
                     preferred_element_type=F32) + bcp_ref[...]

    n_tc = gate_scr.shape[0] // 2
    g_attn = jnp.concatenate([gate_scr[c] for c in range(n_tc)], axis=1)
    g_conv = jnp.concatenate([gate_scr[n_tc + c] for c in range(n_tc)], axis=1)
    merged = (g_attn * ya_scr[...] + g_conv * y_conv).astype(BF16)
    out_ref[...] = x_ref[...] + jnp.dot(merged, wo_ref[...], preferred_element_type=F32)


def _mix_out(x, h, outs, lses, w_tail, b_tail, w_attn_proj, conv_w, conv_b, ln_g, ln_b,
             w_conv_proj, b_conv_proj, w_out, batch, seq, ts):
    t, d = x.shape
    n_ct = d // LANES
    conv_width = conv_w.shape[0]
    tiles = seq // ts
    hb = ts // HALO
    n_hb = seq // HALO
    ext = ts + 2 * HALO
    row = lambda b, i: (b * tiles + i, 0)
    ol_specs = [pl.BlockSpec((None, dil, ts // dil, GROUP_WIDTH), lambda b, i: (b, 0, i, 0))
                for dil in DILATIONS]
    hp_spec = pl.BlockSpec((HALO, d), lambda b, i: (b * n_hb + jnp.maximum(i * hb - 1, 0), 0))
    hn_spec = pl.BlockSpec((HALO, d), lambda b, i: (b * n_hb + jnp.minimum((i + 1) * hb, n_hb - 1), 0))
    cw = jnp.pad(conv_w, ((0, -conv_width % 8), (0, 0)))
    cw = cw.reshape(cw.shape[0], n_ct, LANES).transpose(1, 0, 2)
    consts = [w_tail, b_tail,
              w_attn_proj.astype(BF16), cw, conv_b.reshape(n_ct, 1, LANES), ln_g.reshape(1, d),
              ln_b.reshape(1, d), w_conv_proj.astype(BF16), b_conv_proj.reshape(1, d),
              w_out.astype(BF16)]
    const_bytes = sum(a.size * a.dtype.itemsize for a in consts)
    scratch = [pltpu.VMEM((n_ct * ext, LANES), F32),
               pltpu.VMEM((n_ct * ts, LANES), F32),
               pltpu.VMEM((2 * d // CHUNK, ts, CHUNK), F32),
               pltpu.VMEM((ts, d), F32),
               pltpu.VMEM((4 * (GROUP_WIDTH // LANES) * ts, LANES), F32)]
    scratch_bytes = (ext + ts) * d * 4 + 3 * ts * d * 4 + 4 * ts * GROUP_WIDTH * 4
    vmem = (2 * 2 * N_GROUPS * ts * GROUP_WIDTH * 4 + 4 * ts * d * 4 + 2 * ext * d * 2
            + const_bytes + scratch_bytes
            + 10 * ts * d * 4 + (4 << 20))
    return pl.pallas_call(
        functools.partial(_mix_body, conv_width),
        out_shape=jax.ShapeDtypeStruct((t, d), F32),
        grid=(batch, tiles),
        in_specs=ol_specs + ol_specs + [pl.BlockSpec((ts, d), row)] * 2 + [hp_spec, hn_spec]
                 + [_const_spec(a.shape) for a in consts],
        out_specs=pl.BlockSpec((ts, d), row),
        scratch_shapes=scratch,
        compiler_params=_params(2, vmem),
        name="mix_out",
    )(*outs, *lses, x, h, h, h, *consts)


def kernel(x, ffn1_norm, ffn1_w_up, ffn1_w_down, mix_norm, w_in, b_in, q_norm, k_norm,
           w_attn_proj, conv_w, conv_b, conv_ln_g, conv_ln_b, w_conv_proj, b_conv_proj,
           w_out, ffn2_norm, ffn2_w_up, ffn2_w_down, final_norm):
    batch, seq, d = x.shape
    depth = ffn1_norm.shape[0]
    proj_tm = 512
    rope = _rope_tables(seq, proj_tm)
    h = x.reshape(batch * seq, d)
    for l in range(depth):
        h = _ffn(h, ffn1_norm[l], ffn1_w_up[l], ffn1_w_down[l], None, tm=1024)
        (w_qkv, b_qkv), (w_tail, b_tail) = _split_w_in(w_in[l], b_in[l])
        *qkv, hn = _qkv_proj(h, mix_norm[l], w_qkv, b_qkv, q_norm[l], k_norm[l], rope, batch, seq, proj_tm)
        outs, lses = zip(*[_attn_group(*qkv[3 * g:3 * g + 3], g) for g in range(N_GROUPS)])
        h = _mix_out(h, hn, outs, lses, w_tail, b_tail, w_attn_proj[l], conv_w[l], conv_b[l],
                     conv_ln_g[l], conv_ln_b[l], w_conv_proj[l], b_conv_proj[l], w_out[l],
                     batch, seq, ts=512)
        h = _ffn(h, ffn2_norm[l], ffn2_w_up[l], ffn2_w_down[l], final_norm[l], tm=1024)
    return h.reshape(batch, seq, d)
```

```python
import functools
import math

import numpy as np

import jax
import jax.numpy as jnp
from jax import lax
from jax.experimental import pallas as pl
from jax.experimental.pallas import tpu as pltpu

HEAD_DIM = 64
HEADS_PER_GROUP = 8
DILATION_GROUPS = ((128, 1), (512, 4), (2048, 16))
DILATIONS = tuple(dil for _, dil in DILATION_GROUPS)
N_GROUPS = len(DILATION_GROUPS)
GROUP_WIDTH = HEADS_PER_GROUP * HEAD_DIM
ATTN_WIDTH = N_GROUPS * GROUP_WIDTH
ROPE_THETA = 10000.0
EPS = 1e-6
NEG = -1e30

LANES = 128
CHUNK = 256
VMEM_BYTES = 64 * 1024 * 1024
HALO = 16
Q_BLOCK = 128
UNROLLED_Q_BLOCKS = 4
ATTN_STEP_ROWS = 1024
CONV_ROWS = 128
MAX_ROW_STRIDE = 4
HEADS_PER_CHUNK = CHUNK // HEAD_DIM
CHUNKS_PER_GROUP = GROUP_WIDTH // CHUNK
TILES_PER_CHUNK = CHUNK // LANES
HALF = HEAD_DIM // 2

F32 = jnp.float32
BF16 = jnp.bfloat16


def _rms(x, g):
    return x * lax.rsqrt(jnp.mean(x * x, axis=-1, keepdims=True) + EPS) * g


def _const_spec(shape):
    zeros = (0,) * len(shape)
    return pl.BlockSpec(shape, lambda *_: zeros, pipeline_mode=pl.Buffered(1))


def _params(n_axes, vmem_bytes):
    return pltpu.CompilerParams(
        dimension_semantics=("arbitrary",) * n_axes,
        vmem_limit_bytes=min(int(vmem_bytes), VMEM_BYTES - 4 * 1024 * 1024))


def _ffn_body(final_norm, x_ref, g_ref, wup_ref, wdn_ref, *rest):
    if final_norm:
        fg_ref, o_ref, h_scr, acc_scr = rest
    else:
        o_ref, h_scr, acc_scr = rest
    d_ff = wdn_ref.shape[0]
    h_scr[...] = _rms(x_ref[...], g_ref[...]).astype(BF16)
    for c in range(d_ff // CHUNK):
        h = h_scr[...]
        a = jnp.dot(h, wup_ref[:, c * CHUNK:(c + 1) * CHUNK], preferred_element_type=F32)
        b = jnp.dot(h, wup_ref[:, d_ff + c * CHUNK:d_ff + (c + 1) * CHUNK], preferred_element_type=F32)
        s = (a * jax.nn.sigmoid(a) * b).astype(BF16)
        y = jnp.dot(s, wdn_ref[c * CHUNK:(c + 1) * CHUNK, :], preferred_element_type=F32)
        if c == 0:
            acc_scr[...] = y
        else:
            acc_scr[...] += y
    y = x_ref[...] + 0.5 * acc_scr[...]
    if final_norm:
        y = _rms(y, fg_ref[...])
    o_ref[...] = y


def _ffn(x, norm_g, w_up, w_down, final_g, tm):
    t, d = x.shape
    d_ff = w_down.shape[0]
    final_norm = final_g is not None
    args = [x, norm_g.reshape(1, d), w_up.astype(BF16), w_down.astype(BF16)]
    specs = [pl.BlockSpec((tm, d), lambda i: (i, 0)), _const_spec((1, d)),
             _const_spec(w_up.shape), _const_spec(w_down.shape)]
    if final_norm:
        args.append(final_g.reshape(1, d))
        specs.append(_const_spec((1, d)))
    vmem = (4 * tm * d * 4 + 3 * d_ff * d * 2 + tm * d * 6 + 8 * tm * CHUNK * 4 + (8 << 20))
    return pl.pallas_call(
        functools.partial(_ffn_body, final_norm),
        out_shape=jax.ShapeDtypeStruct((t, d), F32),
        grid=(t // tm,),
        in_specs=specs,
        out_specs=pl.BlockSpec((tm, d), lambda i: (i, 0)),
        scratch_shapes=[pltpu.VMEM((tm, d), BF16), pltpu.VMEM((tm, d), F32)],
        compiler_params=_params(1, vmem),
        name="ffn_final" if final_norm else "ffn",
    )(*args)


def _rotary_cols(a):
    lead = a.shape[:-1]
    a = a.reshape(*lead, -1, HEADS_PER_CHUNK, TILES_PER_CHUNK, HALF)
    return jnp.swapaxes(a, -2, -3).reshape(*lead, -1)


def _split_w_in(w_in, b_in):
    def qkv(a):
        return jnp.concatenate([_rotary_cols(a[..., :ATTN_WIDTH]),
                                _rotary_cols(a[..., ATTN_WIDTH:2 * ATTN_WIDTH]),
                                a[..., 2 * ATTN_WIDTH:3 * ATTN_WIDTH]], axis=-1)
    b_in = b_in.reshape(1, -1)
    return ((qkv(w_in).astype(BF16), qkv(b_in)),
            (w_in[:, 3 * ATTN_WIDTH:].astype(BF16), b_in[:, 3 * ATTN_WIDTH:]))


def _head_gain(g):
    full = jnp.broadcast_to(g[:, None, :], (N_GROUPS, HEADS_PER_GROUP, HEAD_DIM))
    return _rotary_cols(full.reshape(1, ATTN_WIDTH))


def _rope_tables(seq, tm):
    inv = ROPE_THETA ** (-jnp.arange(0, HEAD_DIM, 2, dtype=F32) / HEAD_DIM)
    tables = []
    for dil in DILATIONS:
        row = np.arange(seq)
        tile, r, m = row // tm, (row % tm) // (tm // dil), row % (tm // dil)
        pos = jnp.asarray(tile * tm + m * dil + r, F32)
        ang = pos[:, None] * inv[None, :]
        cos = jnp.tile(jnp.cos(ang), (1, LANES // HALF))
        sin = jnp.tile(jnp.sin(ang), (1, LANES // HALF))
        scale = 1.0 / math.sqrt(HEAD_DIM)
        tables.append(jnp.stack([cos * scale, sin * scale, cos, sin]))
    return jnp.stack(tables)


def _segment_matrix():
    lane = np.arange(LANES)
    same = (lane[:, None] // HALF == lane[None, :] // HALF).astype(np.float32)
    return jnp.asarray(np.tile(same, (2, TILES_PER_CHUNK)), BF16)


def _head_norm_rope(z, seg_ref, gain, cos, sin):
    z0, z1 = z[:, :LANES], z[:, LANES:]
    w = z0 * z0 + z1 * z1
    w_hi = w.astype(BF16)
    w_lo = (w - w_hi.astype(F32)).astype(BF16)
    ss = jnp.dot(jnp.concatenate([w_hi, w_lo], axis=1), seg_ref[...], preferred_element_type=F32)
    t = z * lax.rsqrt(ss * (1.0 / HEAD_DIM) + EPS) * gain
    t0, t1 = t[:, :LANES], t[:, LANES:]
    return jnp.concatenate([t0 * cos - t1 * sin, t1 * cos + t0 * sin], axis=1)


def _qkv_body(x_ref, g_ref, w_ref, b_ref, gq_ref, gk_ref, rope_ref, seg_ref, *rest):
    qkv_refs = rest[:3 * N_GROUPS]
    h_ref, hn_scr = rest[3 * N_GROUPS:3 * N_GROUPS + 2]
    h_scrs = (h_ref,) + rest[3 * N_GROUPS + 2:]
    tm = x_ref.shape[0]

    hn = _rms(x_ref[...], g_ref[...])
    h_ref[...] = hn.astype(BF16)
    lane_tiles = hn_scr.shape[0] // tm
    for j in range(lane_tiles):
        hn_scr[j * tm:(j + 1) * tm, :] = hn[:, j * LANES:(j + 1) * LANES]
    for dil, h_scr in zip(DILATIONS[1:], h_scrs[1:]):
        rows = tm // dil
        for r in range(dil):
            for j in range(lane_tiles):
                src = hn_scr[pl.ds(j * tm + r, rows, stride=dil), :]
                h_scr[r * rows:(r + 1) * rows, j * LANES:(j + 1) * LANES] = src.astype(BF16)

    for g, dil in enumerate(DILATIONS):
        q_ref, k_ref, v_ref = qkv_refs[3 * g:3 * g + 3]
        h = h_scrs[g][...]
        shape = (dil, tm // dil, CHUNK)
        for c in range(CHUNKS_PER_GROUP):
            cols = slice((g * CHUNKS_PER_GROUP + c) * CHUNK, (g * CHUNKS_PER_GROUP + c + 1) * CHUNK)
            z = [jnp.dot(h, w_ref[:, n * ATTN_WIDTH + cols.start:n * ATTN_WIDTH + cols.stop],
                         preferred_element_type=F32)
                 + b_ref[:, n * ATTN_WIDTH + cols.start:n * ATTN_WIDTH + cols.stop] for n in range(3)]
            q = _head_norm_rope(z[0], seg_ref, gq_ref[:, cols], rope_ref[g, 0], rope_ref[g, 1])
            q_ref[c] = q.astype(BF16).reshape(shape)
            k = _head_norm_rope(z[1], seg_ref, gk_ref[:, cols], rope_ref[g, 2], rope_ref[g, 3])
            k_ref[c] = k.astype(BF16).reshape(shape)
            v_ref[c] = z[2].astype(BF16).reshape(shape)


def _qkv_proj(x, mix_g, w_qkv, b_qkv, q_norm, k_norm, rope, batch, seq, tm):
    t, d = x.shape
    gq, gk = _head_gain(q_norm), _head_gain(k_norm)
    seg = _segment_matrix()
    tiles = seq // tm
    consts = [w_qkv, b_qkv, gq, gk]
    args = [x, mix_g.reshape(1, d), *consts, rope, seg]
    specs = ([pl.BlockSpec((tm, d), lambda i, b: (b * tiles + i, 0)), _const_spec((1, d))]
             + [_const_spec(a.shape) for a in consts]
             + [pl.BlockSpec((N_GROUPS, 4, tm, LANES), lambda i, b: (0, 0, i, 0)),
                _const_spec(seg.shape)])
    out_shape, out_specs = [], []
    for dil in DILATIONS:
        shape = (CHUNKS_PER_GROUP, batch, dil, seq // dil, CHUNK)
        spec = pl.BlockSpec((CHUNKS_PER_GROUP, None, dil, tm // dil, CHUNK),
                            lambda i, b: (0, b, 0, i, 0))
        out_shape += [jax.ShapeDtypeStruct(shape, BF16)] * 3
        out_specs += [spec] * 3
    out_shape.append(jax.ShapeDtypeStruct((t, d), BF16))
    out_specs.append(pl.BlockSpec((tm, d), lambda i, b: (b * tiles + i, 0)))
    vmem = (2 * tm * d * 4 + 3 * ATTN_WIDTH * d * 2 + 2 * 3 * ATTN_WIDTH * tm * 2
            + tm * d * (4 + 2 * N_GROUPS) + 2 * N_GROUPS * 4 * tm * LANES * 4
            + 8 * tm * CHUNK * 4 + (6 << 20))
    return pl.pallas_call(
        _qkv_body,
        out_shape=out_shape,
        grid=(tiles, batch),
        in_specs=specs,
        out_specs=out_specs,
        scratch_shapes=[pltpu.VMEM((d // LANES * tm, LANES), F32)] + [pltpu.VMEM((tm, d), BF16)] * (N_GROUPS - 1),
        compiler_params=_params(2, vmem),
        name="qkv_proj",
    )(*args)


def _attn_body(sub_len, half_window, q_ref, k_ref, v_ref, o_ref, l_ref):
    n_sub, qt_rows = q_ref.shape[1:3]
    blocks_per_sub = qt_rows // Q_BLOCK
    key_rows = Q_BLOCK + 2 * half_window
    tile_start = pl.program_id(2) * qt_rows
    lane = lax.broadcasted_iota(jnp.int32, (1, CHUNK), 1)
    head_lanes = [((lane % LANES) // HALF == hh).astype(BF16) for hh in range(HEADS_PER_CHUNK)]
    v_head = lax.broadcasted_iota(jnp.int32, (Q_BLOCK, CHUNK), 1) // HEAD_DIM
    offset = (lax.broadcasted_iota(jnp.int32, (Q_BLOCK, key_rows), 0)
              - lax.broadcasted_iota(jnp.int32, (Q_BLOCK, key_rows), 1))

    def block(n, carry):
        sub = n // blocks_per_sub
        r0 = pl.multiple_of((n % blocks_per_sub) * Q_BLOCK, Q_BLOCK)
        q0 = tile_start + r0
        start = pl.multiple_of(jnp.clip(q0 - half_window, 0, sub_len - key_rows), half_window)
        band = jnp.where(jnp.abs(offset + (q0 - start)) <= half_window, 0.0, NEG)
        for c in range(CHUNKS_PER_GROUP):
            qb = q_ref[c, sub, pl.ds(r0, Q_BLOCK), :]
            kw = k_ref[c, sub, pl.ds(start, key_rows), :]
            vw = v_ref[c, sub, pl.ds(start, key_rows), :]
            qs = jnp.concatenate([qb * head_lanes[hh] for hh in range(HEADS_PER_CHUNK)], axis=0)
            s = lax.dot_general(qs, kw, (((1,), (1,)), ((), ())), preferred_element_type=F32)
            ps, maxes, dens = [], [], []
            for hh in range(HEADS_PER_CHUNK):
                sh = s[hh * Q_BLOCK:(hh + 1) * Q_BLOCK] + band
                m = jnp.max(sh, axis=-1, keepdims=True)
                p = jnp.exp(sh - m)
                ps.append(p.astype(BF16))
                maxes.append(m)
                dens.append(jnp.sum(p, axis=-1, keepdims=True))
            pv = jnp.dot(jnp.concatenate(ps, axis=0), vw, preferred_element_type=F32)
            o, mx, dn = pv[:Q_BLOCK], maxes[0], dens[0]
            for hh in range(1, HEADS_PER_CHUNK):
                mine = v_head == hh
                o = jnp.where(mine, pv[hh * Q_BLOCK:(hh + 1) * Q_BLOCK], o)
                mx = jnp.where(mine, maxes[hh], mx)
                dn = jnp.where(mine, dens[hh], dn)
            cols = slice(c * CHUNK, (c + 1) * CHUNK)
            o_ref[sub, pl.ds(r0, Q_BLOCK), cols] = o / dn
            l_ref[sub, pl.ds(r0, Q_BLOCK), cols] = mx + jnp.log(dn)
        return carry

    lax.fori_loop(0, n_sub * blocks_per_sub, block, 0, unroll=UNROLLED_Q_BLOCKS)


def _attn_group(q, k, v, g):
    window, dil = DILATION_GROUPS[g]
    half_window = window // (2 * dil)
    _, batch, _, sub_len, _ = q.shape
    qt = min(ATTN_STEP_ROWS, sub_len)
    n_sub = ATTN_STEP_ROWS // qt
    q_spec = pl.BlockSpec((CHUNKS_PER_GROUP, None, n_sub, qt, CHUNK), lambda b, r, t: (0, b, r, t, 0))
    kv_spec = pl.BlockSpec((CHUNKS_PER_GROUP, None, n_sub, sub_len, CHUNK),
                           lambda b, r, t: (0, b, r, 0, 0))
    o_spec = pl.BlockSpec((None, n_sub, qt, GROUP_WIDTH), lambda b, r, t: (b, r, t, 0))
    o_shape = jax.ShapeDtypeStruct((batch, dil, sub_len, GROUP_WIDTH), F32)
    vmem = (2 * CHUNKS_PER_GROUP * ATTN_STEP_ROWS * CHUNK * 2
            + 4 * CHUNKS_PER_GROUP * n_sub * sub_len * CHUNK * 2
            + 4 * ATTN_STEP_ROWS * GROUP_WIDTH * 4 + (8 << 20))
    return pl.pallas_call(
        functools.partial(_attn_body, sub_len, half_window),
        out_shape=[o_shape, o_shape],
        grid=(batch, dil // n_sub, sub_len // qt),
        in_specs=[q_spec, kv_spec, kv_spec],
        out_specs=[o_spec, o_spec],
        compiler_params=_params(3, vmem),
        name=f"attn_d{dil}",
    )(q, k, v)


def _mix_body(conv_width, *refs):
    ol_refs = refs[:2 * N_GROUPS]
    (x_ref, h_ref, hp_ref, hn_ref, w_ref, b_ref,
     wap_ref, cw_ref, cb_ref, lg_ref, lb_ref, wcp_ref, bcp_ref, wo_ref,
     out_ref, ue_scr, cv_scr, gate_scr, nat_scr, tmp_scr) = refs[2 * N_GROUPS:]
    i = pl.program_id(1)
    ts, d = x_ref.shape
    ext = ts + 2 * HALO
    n_ct = d // LANES
    pad = conv_width // 2

    row = lax.broadcasted_iota(jnp.int32, (ext, 1), 0)
    inside = (((row >= HALO) | (i > 0)) & ((row < HALO + ts) | (i < pl.num_programs(1) - 1)))
    h_tile = h_ref[...]
    h_all = jnp.concatenate([hp_ref[...], h_tile, hn_ref[...]], axis=0)

    def proj(h, c):
        cols = slice(c * CHUNK, (c + 1) * CHUNK)
        return jnp.dot(h, w_ref[:, cols], preferred_element_type=F32) + b_ref[:, cols]

    glu_chunks = d // CHUNK
    for c in range(glu_chunks):
        u = jnp.where(inside, proj(h_all, c) * jax.nn.sigmoid(proj(h_all, glu_chunks + c)), 0.0)
        for j in range(TILES_PER_CHUNK):
            ct = c * TILES_PER_CHUNK + j
            ue_scr[ct * ext:(ct + 1) * ext, :] = u[:, j * LANES:(j + 1) * LANES]

    def conv_tile(ct, carry):
        src = ct * ext + (HALO - pad)
        dst = pl.multiple_of(ct * ts, CONV_ROWS)
        for rb in range(ts // CONV_ROWS):
            acc = jnp.broadcast_to(cb_ref[ct], (CONV_ROWS, LANES))
            for j in range(conv_width):
                taps = ue_scr[pl.ds(src + rb * CONV_ROWS + j, CONV_ROWS), :]
                acc = acc + taps * cw_ref[ct, j:j + 1, :]
            cv_scr[pl.ds(dst + rb * CONV_ROWS, CONV_ROWS), :] = acc
        return carry

    lax.fori_loop(0, n_ct, conv_tile, 0)

    for c in range(gate_scr.shape[0]):
        gate_scr[c] = jax.nn.sigmoid(proj(h_tile, 2 * glu_chunks + c))

    lane_tiles = GROUP_WIDTH // LANES
    natural = []
    for n, (ref, dil) in enumerate(zip(ol_refs, DILATIONS + DILATIONS)):
        if dil == 1:
            natural.append(ref[0])
            continue
        slot = (n % N_GROUPS - 1) * 2 + n // N_GROUPS
        fine = min(dil, MAX_ROW_STRIDE)
        coarse = dil // fine
        for j in range(lane_tiles):
            base = (slot * lane_tiles + j) * ts
            lanes = slice(j * LANES, (j + 1) * LANES)
            if coarse == 1:
                for r in range(dil):
                    nat_scr[pl.ds(base + r, ts // dil, stride=dil), :] = ref[r, :, lanes]
                continue
            part = ts // fine
            for a in range(fine):
                for b in range(coarse):
                    tmp_scr[pl.ds(base + a * part + b, ts // dil, stride=coarse), :] = ref[a + fine * b, :, lanes]
            for a in range(fine):
                nat_scr[pl.ds(base + a, part, stride=fine), :] = tmp_scr[base + a * part:base + (a + 1) * part, :]
        natural.append(jnp.concatenate(
            [nat_scr[(slot * lane_tiles + j) * ts:(slot * lane_tiles + j + 1) * ts, :]
             for j in range(lane_tiles)], axis=1))
    o0, o1, o2, l0, l1, l2 = natural
    m = jnp.maximum(jnp.maximum(l0, l1), l2)
    e0, e1, e2 = jnp.exp(l0 - m), jnp.exp(l1 - m), jnp.exp(l2 - m)
    y = (e0 * o0 + e1 * o1 + e2 * o2) / (e0 + e1 + e2)
    y_attn = jnp.dot(y.astype(BF16), wap_ref[...], preferred_element_type=F32)

    conv = [cv_scr[ct * ts:(ct + 1) * ts, :] for ct in range(n_ct)]
    row_sum = jnp.zeros((ts, 1), F32)
    for ct in range(n_ct):
        row_sum = row_sum + jnp.sum(conv[ct], axis=-1, keepdims=True)
    mu = row_sum * (1.0 / d)
    sq_sum = jnp.zeros((ts, 1), F32)
    for ct in range(n_ct):
        dev = conv[ct] - mu
        sq_sum = sq_sum + jnp.sum(dev * dev, axis=-1, keepdims=True)
    rstd = lax.rsqrt(sq_sum * (1.0 / d) + EPS)
    acts = []
    for ct in range(n_ct):
        cols = slice(ct * LANES, (ct + 1) * LANES)
        ln = (conv[ct] - mu) * rstd * lg_ref[:, cols] + lb_ref[:, cols]
        acts.append((ln * jax.nn.sigmoid(ln)).astype(BF16))
    y_conv = jnp.dot(jnp.concatenate(acts, axis=1), wcp_ref[...],
                     preferred_element_type=F32) + bcp_ref[...]

    n_tc = gate_scr.shape[0] // 2
    g_attn = jnp.concatenate([gate_scr[c] for c in range(n_tc)], axis=1)
    g_conv = jnp.concatenate([gate_scr[n_tc + c] for c in range(n_tc)], axis=1)
    merged = (g_attn * y_attn + g_conv * y_conv).astype(BF16)
    out_ref[...] = x_ref[...] + jnp.dot(merged, wo_ref[...], preferred_element_type=F32)


def _mix_out(x, h, outs, lses, w_tail, b_tail, w_attn_proj, conv_w, conv_b, ln_g, ln_b,
             w_conv_proj, b_conv_proj, w_out, batch, seq, ts):
    t, d = x.shape
    n_ct = d // LANES
    conv_width = conv_w.shape[0]
    tiles = seq // ts
    hb = ts // HALO
    n_hb = seq // HALO
    ext = ts + 2 * HALO
    row = lambda b, i: (b * tiles + i, 0)
    ol_specs = [pl.BlockSpec((None, dil, ts // dil, GROUP_WIDTH), lambda b, i: (b, 0, i, 0))
                for dil in DILATIONS]
    hp_spec = pl.BlockSpec((HALO, d), lambda b, i: (b * n_hb + jnp.maximum(i * hb - 1, 0), 0))
    hn_spec = pl.BlockSpec((HALO, d), lambda b, i: (b * n_hb + jnp.minimum((i + 1) * hb, n_hb - 1), 0))
    cw = jnp.pad(conv_w, ((0, -conv_width % 8), (0, 0)))
    cw = cw.reshape(cw.shape[0], n_ct, LANES).transpose(1, 0, 2)
    consts = [w_tail, b_tail,
              w_attn_proj.astype(BF16), cw, conv_b.reshape(n_ct, 1, LANES), ln_g.reshape(1, d),
              ln_b.reshape(1, d), w_conv_proj.astype(BF16), b_conv_proj.reshape(1, d),
              w_out.astype(BF16)]
    const_bytes = sum(a.size * a.dtype.itemsize for a in consts)
    scratch = [pltpu.VMEM((n_ct * ext, LANES), F32),
               pltpu.VMEM((n_ct * ts, LANES), F32),
               pltpu.VMEM((2 * d // CHUNK, ts, CHUNK), F32),
               pltpu.VMEM((4 * (GROUP_WIDTH // LANES) * ts, LANES), F32),
               pltpu.VMEM((4 * (GROUP_WIDTH // LANES) * ts, LANES), F32)]
    scratch_bytes = (ext + ts) * d * 4 + 2 * ts * d * 4 + 8 * ts * GROUP_WIDTH * 4
    vmem = (2 * 2 * N_GROUPS * ts * GROUP_WIDTH * 4 + 4 * ts * d * 4 + 2 * ext * d * 2
            + const_bytes + scratch_bytes
            + 10 * ts * d * 4 + (4 << 20))
    return pl.pallas_call(
        functools.partial(_mix_body, conv_width),
        out_shape=jax.ShapeDtypeStruct((t, d), F32),
        grid=(batch, tiles),
        in_specs=ol_specs + ol_specs + [pl.BlockSpec((ts, d), row)] * 2 + [hp_spec, hn_spec]
                 + [_const_spec(a.shape) for a in consts],
        out_specs=pl.BlockSpec((ts, d), row),
        scratch_shapes=scratch,
        compiler_params=_params(2, vmem),
        name="mix_out",
    )(*outs, *lses, x, h, h, h, *consts)


def kernel(x, ffn1_norm, ffn1_w_up, ffn1_w_down, mix_norm, w_in, b_in, q_norm, k_norm,
           w_attn_proj, conv_w, conv_b, conv_ln_g, conv_ln_b, w_conv_proj, b_conv_proj,
           w_out, ffn2_norm, ffn2_w_up, ffn2_w_down, final_norm):
    batch, seq, d = x.shape
    depth = ffn1_norm.shape[0]
    proj_tm = 512
    rope = _rope_tables(seq, proj_tm)
    h = x.reshape(batch * seq, d)
    for l in range(depth):
        h = _ffn(h, ffn1_norm[l], ffn1_w_up[l], ffn1_w_down[l], None, tm=1024)
        (w_qkv, b_qkv), (w_tail, b_tail) = _split_w_in(w_in[l], b_in[l])
        *qkv, hn = _qkv_proj(h, mix_norm[l], w_qkv, b_qkv, q_norm[l], k_norm[l], rope, batch, seq, proj_tm)
        outs, lses = zip(*[_attn_group(*qkv[3 * g:3 * g + 3], g) for g in range(N_GROUPS)])
        h = _mix_out(h, hn, outs, lses, w_tail, b_tail, w_attn_proj[l], conv_w[l], conv_b[l],
                     conv_ln_g[l], conv_ln_b[l], w_conv_proj[l], b_conv_proj[l], w_out[l],
                     batch, seq, ts=512)
        h = _ffn(h, ffn2_norm[l], ffn2_w_up[l], ffn2_w_down[l], final_norm[l], tm=1024)
    return h.reshape(batch, seq, d)
```

```python
import functools
import math

import numpy as np

import jax
import jax.numpy as jnp
from jax import lax
from jax.experimental import pallas as pl
from jax.experimental.pallas import tpu as pltpu

HEAD_DIM = 64
HEADS_PER_GROUP = 8
DILATION_GROUPS = ((128, 1), (512, 4), (2048, 16))
DILATIONS = tuple(dil for _, dil in DILATION_GROUPS)
N_GROUPS = len(DILATION_GROUPS)
GROUP_WIDTH = HEADS_PER_GROUP * HEAD_DIM
ATTN_WIDTH = N_GROUPS * GROUP_WIDTH
ROPE_THETA = 10000.0
EPS = 1e-6
NEG = -1e30

LANES = 128
CHUNK = 256
VMEM_BYTES = 64 * 1024 * 1024
HALO = 16
Q_BLOCK = 128
UNROLLED_Q_BLOCKS = 8
ATTN_STEP_ROWS = 1024
CONV_ROWS = 128
MAX_ROW_STRIDE = 4
HEADS_PER_CHUNK = CHUNK // HEAD_DIM
CHUNKS_PER_GROUP = GROUP_WIDTH // CHUNK
TILES_PER_CHUNK = CHUNK // LANES
HALF = HEAD_DIM // 2

F32 = jnp.float32
BF16 = jnp.bfloat16


def _rms(x, g):
    return x * lax.rsqrt(jnp.mean(x * x, axis=-1, keepdims=True) + EPS) * g


def _const_spec(shape):
    zeros = (0,) * len(shape)
    return pl.BlockSpec(shape, lambda *_: zeros, pipeline_mode=pl.Buffered(1))


def _layer_spec(stacked_shape, layer):
    index = (layer,) + (0,) * (len(stacked_shape) - 1)
    return pl.BlockSpec((None,) + tuple(stacked_shape[1:]), lambda *_: index,
                        pipeline_mode=pl.Buffered(1))


def _params(n_axes, vmem_bytes):
    return pltpu.CompilerParams(
        dimension_semantics=("arbitrary",) * n_axes,
        vmem_limit_bytes=min(int(vmem_bytes), VMEM_BYTES - 4 * 1024 * 1024))


def _ffn_body(final_norm, x_ref, g_ref, wup_ref, wdn_ref, *rest):
    if final_norm:
        fg_ref, o_ref, h_scr, acc_scr = rest
    else:
        o_ref, h_scr, acc_scr = rest
    d_ff = wdn_ref.shape[0]
    h_scr[...] = _rms(x_ref[...], g_ref[...]).astype(BF16)
    for c in range(d_ff // CHUNK):
        h = h_scr[...]
        a = jnp.dot(h, wup_ref[:, c * CHUNK:(c + 1) * CHUNK], preferred_element_type=F32)
        b = jnp.dot(h, wup_ref[:, d_ff + c * CHUNK:d_ff + (c + 1) * CHUNK], preferred_element_type=F32)
        s = (a * jax.nn.sigmoid(a) * b).astype(BF16)
        y = jnp.dot(s, wdn_ref[c * CHUNK:(c + 1) * CHUNK, :], preferred_element_type=F32)
        if c == 0:
            acc_scr[...] = y
        else:
            acc_scr[...] += y
    y = x_ref[...] + 0.5 * acc_scr[...]
    if final_norm:
        y = _rms(y, fg_ref[...])
    o_ref[...] = y


def _ffn(x, layer, norm_g, w_up, w_down, final_g, tm):
    t, d = x.shape
    d_ff = w_down.shape[1]
    final_norm = final_g is not None
    args = [x, norm_g, w_up, w_down] + ([final_g] if final_norm else [])
    specs = ([pl.BlockSpec((tm, d), lambda i: (i, 0))]
             + [_layer_spec(a.shape, layer) for a in args[1:]])
    vmem = (4 * tm * d * 4 + 3 * d_ff * d * 2 + tm * d * 6 + 8 * tm * CHUNK * 4 + (8 << 20))
    return pl.pallas_call(
        functools.partial(_ffn_body, final_norm),
        out_shape=jax.ShapeDtypeStruct((t, d), F32),
        grid=(t // tm,),
        in_specs=specs,
        out_specs=pl.BlockSpec((tm, d), lambda i: (i, 0)),
        scratch_shapes=[pltpu.VMEM((tm, d), BF16), pltpu.VMEM((tm, d), F32)],
        compiler_params=_params(1, vmem),
        name="ffn_final" if final_norm else "ffn",
    )(*args)


def _rotary_cols(a):
    lead = a.shape[:-1]
    a = a.reshape(*lead, -1, HEADS_PER_CHUNK, TILES_PER_CHUNK, HALF)
    return jnp.swapaxes(a, -2, -3).reshape(*lead, -1)


def _split_w_in(w_in, b_in):
    def qkv(a):
        return jnp.concatenate([_rotary_cols(a[..., :ATTN_WIDTH]),
                                _rotary_cols(a[..., ATTN_WIDTH:2 * ATTN_WIDTH]),
                                a[..., 2 * ATTN_WIDTH:3 * ATTN_WIDTH]], axis=-1)
    b_in = b_in[:, None, :]
    return ((qkv(w_in).astype(BF16), qkv(b_in)),
            (w_in[..., 3 * ATTN_WIDTH:].astype(BF16), b_in[..., 3 * ATTN_WIDTH:]))


def _head_gain(g):
    layers = g.shape[0]
    full = jnp.broadcast_to(g[:, :, None, :], (layers, N_GROUPS, HEADS_PER_GROUP, HEAD_DIM))
    return _rotary_cols(full.reshape(layers, 1, ATTN_WIDTH))


def _rope_tables(seq, tm):
    inv = ROPE_THETA ** (-jnp.arange(0, HEAD_DIM, 2, dtype=F32) / HEAD_DIM)
    tables = []
    for dil in DILATIONS:
        row = np.arange(seq)
        tile, r, m = row // tm, (row % tm) // (tm // dil), row % (tm // dil)
        pos = jnp.asarray(tile * tm + m * dil + r, F32)
        ang = pos[:, None] * inv[None, :]
        cos = jnp.tile(jnp.cos(ang), (1, LANES // HALF))
        sin = jnp.tile(jnp.sin(ang), (1, LANES // HALF))
        scale = 1.0 / math.sqrt(HEAD_DIM)
        tables.append(jnp.stack([cos * scale, sin * scale, cos, sin]))
    return jnp.stack(tables)


def _segment_matrix():
    lane = np.arange(LANES)
    same = (lane[:, None] // HALF == lane[None, :] // HALF).astype(np.float32)
    return jnp.asarray(np.tile(same, (2, TILES_PER_CHUNK)), BF16)


def _head_norm_rope(z, seg_ref, gain, cos, sin):
    z0, z1 = z[:, :LANES], z[:, LANES:]
    w = z0 * z0 + z1 * z1
    w_hi = w.astype(BF16)
    w_lo = (w - w_hi.astype(F32)).astype(BF16)
    ss = jnp.dot(jnp.concatenate([w_hi, w_lo], axis=1), seg_ref[...], preferred_element_type=F32)
    t = z * lax.rsqrt(ss * (1.0 / HEAD_DIM) + EPS) * gain
    t0, t1 = t[:, :LANES], t[:, LANES:]
    return jnp.concatenate([t0 * cos - t1 * sin, t1 * cos + t0 * sin], axis=1)


def _qkv_body(x_ref, g_ref, w_ref, b_ref, gq_ref, gk_ref, rope_ref, seg_ref, *rest):
    qkv_refs = rest[:3 * N_GROUPS]
    h_ref, hn_scr = rest[3 * N_GROUPS:3 * N_GROUPS + 2]
    h_scrs = (h_ref,) + rest[3 * N_GROUPS + 2:]
    tm = x_ref.shape[0]

    hn = _rms(x_ref[...], g_ref[...])
    h_ref[...] = hn.astype(BF16)
    lane_tiles = hn_scr.shape[0] // tm
    for j in range(lane_tiles):
        hn_scr[j * tm:(j + 1) * tm, :] = hn[:, j * LANES:(j + 1) * LANES]
    for dil, h_scr in zip(DILATIONS[1:], h_scrs[1:]):
        rows = tm // dil
        for r in range(dil):
            for j in range(lane_tiles):
                src = hn_scr[pl.ds(j * tm + r, rows, stride=dil), :]
                h_scr[r * rows:(r + 1) * rows, j * LANES:(j + 1) * LANES] = src.astype(BF16)

    for g, dil in enumerate(DILATIONS):
        q_ref, k_ref, v_ref = qkv_refs[3 * g:3 * g + 3]
        h = h_scrs[g][...]
        shape = (dil, tm // dil, CHUNK)
        for c in range(CHUNKS_PER_GROUP):
            cols = slice((g * CHUNKS_PER_GROUP + c) * CHUNK, (g * CHUNKS_PER_GROUP + c + 1) * CHUNK)
            z = [jnp.dot(h, w_ref[:, n * ATTN_WIDTH + cols.start:n * ATTN_WIDTH + cols.stop],
                         preferred_element_type=F32)
                 + b_ref[:, n * ATTN_WIDTH + cols.start:n * ATTN_WIDTH + cols.stop] for n in range(3)]
            q = _head_norm_rope(z[0], seg_ref, gq_ref[:, cols], rope_ref[g, 0], rope_ref[g, 1])
            q_ref[c] = q.astype(BF16).reshape(shape)
            k = _head_norm_rope(z[1], seg_ref, gk_ref[:, cols], rope_ref[g, 2], rope_ref[g, 3])
            k_ref[c] = k.astype(BF16).reshape(shape)
            v_ref[c] = z[2].astype(BF16).reshape(shape)


def _qkv_proj(x, layer, mix_g, w_qkv, b_qkv, gq, gk, rope, batch, seq, tm):
    t, d = x.shape
    seg = _segment_matrix()
    tiles = seq // tm
    consts = [mix_g, w_qkv, b_qkv, gq, gk]
    args = [x, *consts, rope, seg]
    specs = ([pl.BlockSpec((tm, d), lambda i, b: (b * tiles + i, 0))]
             + [_layer_spec(a.shape, layer) for a in consts]
             + [pl.BlockSpec((N_GROUPS, 4, tm, LANES), lambda i, b: (0, 0, i, 0)),
                _const_spec(seg.shape)])
    out_shape, out_specs = [], []
    for dil in DILATIONS:
        shape = (CHUNKS_PER_GROUP, batch, dil, seq // dil, CHUNK)
        spec = pl.BlockSpec((CHUNKS_PER_GROUP, None, dil, tm // dil, CHUNK),
                            lambda i, b: (0, b, 0, i, 0))
        out_shape += [jax.ShapeDtypeStruct(shape, BF16)] * 3
        out_specs += [spec] * 3
    out_shape.append(jax.ShapeDtypeStruct((t, d), BF16))
    out_specs.append(pl.BlockSpec((tm, d), lambda i, b: (b * tiles + i, 0)))
    vmem = (2 * tm * d * 4 + 3 * ATTN_WIDTH * d * 2 + 2 * 3 * ATTN_WIDTH * tm * 2
            + tm * d * (4 + 2 * N_GROUPS) + 2 * N_GROUPS * 4 * tm * LANES * 4
            + 8 * tm * CHUNK * 4 + (6 << 20))
    return pl.pallas_call(
        _qkv_body,
        out_shape=out_shape,
        grid=(tiles, batch),
        in_specs=specs,
        out_specs=out_specs,
        scratch_shapes=[pltpu.VMEM((d // LANES * tm, LANES), F32)] + [pltpu.VMEM((tm, d), BF16)] * (N_GROUPS - 1),
        compiler_params=_params(2, vmem),
        name="qkv_proj",
    )(*args)


def _attn_body(sub_len, half_window, q_ref, k_ref, v_ref, o_ref, l_ref):
    n_sub, qt_rows = q_ref.shape[1:3]
    blocks_per_sub = qt_rows // Q_BLOCK
    key_rows = Q_BLOCK + 2 * half_window
    tile_start = pl.program_id(2) * qt_rows
    lane = lax.broadcasted_iota(jnp.int32, (1, CHUNK), 1)
    head_lanes = [((lane % LANES) // HALF == hh).astype(BF16) for hh in range(HEADS_PER_CHUNK)]
    v_head = lax.broadcasted_iota(jnp.int32, (Q_BLOCK, CHUNK), 1) // HEAD_DIM
    offset = (lax.broadcasted_iota(jnp.int32, (Q_BLOCK, key_rows), 0)
              - lax.broadcasted_iota(jnp.int32, (Q_BLOCK, key_rows), 1))

    def block(n, carry):
        sub = n // blocks_per_sub
        r0 = pl.multiple_of((n % blocks_per_sub) * Q_BLOCK, Q_BLOCK)
        q0 = tile_start + r0
        start = pl.multiple_of(jnp.clip(q0 - half_window, 0, sub_len - key_rows), half_window)
        band = jnp.where(jnp.abs(offset + (q0 - start)) <= half_window, 0.0, NEG)
        for c in range(CHUNKS_PER_GROUP):
            qb = q_ref[c, sub, pl.ds(r0, Q_BLOCK), :]
            kw = k_ref[c, sub, pl.ds(start, key_rows), :]
            vw = v_ref[c, sub, pl.ds(start, key_rows), :]
            qs = jnp.concatenate([qb * head_lanes[hh] for hh in range(HEADS_PER_CHUNK)], axis=0)
            s = lax.dot_general(qs, kw, (((1,), (1,)), ((), ())), preferred_element_type=F32)
            ps, maxes, dens = [], [], []
            for hh in range(HEADS_PER_CHUNK):
                sh = s[hh * Q_BLOCK:(hh + 1) * Q_BLOCK] + band
                m = jnp.max(sh, axis=-1, keepdims=True)
                p = jnp.exp(sh - m)
                ps.append(p.astype(BF16))
                maxes.append(m)
                dens.append(jnp.sum(p, axis=-1, keepdims=True))
            pv = jnp.dot(jnp.concatenate(ps, axis=0), vw, preferred_element_type=F32)
            o, mx, dn = pv[:Q_BLOCK], maxes[0], dens[0]
            for hh in range(1, HEADS_PER_CHUNK):
                mine = v_head == hh
                o = jnp.where(mine, pv[hh * Q_BLOCK:(hh + 1) * Q_BLOCK], o)
                mx = jnp.where(mine, maxes[hh], mx)
                dn = jnp.where(mine, dens[hh], dn)
            cols = slice(c * CHUNK, (c + 1) * CHUNK)
            o_ref[sub, pl.ds(r0, Q_BLOCK), cols] = o / dn
            l_ref[sub, pl.ds(r0, Q_BLOCK), cols] = mx + jnp.log(dn)
        return carry

    lax.fori_loop(0, n_sub * blocks_per_sub, block, 0, unroll=UNROLLED_Q_BLOCKS)


def _attn_group(q, k, v, g):
    window, dil = DILATION_GROUPS[g]
    half_window = window // (2 * dil)
    _, batch, _, sub_len, _ = q.shape
    qt = min(ATTN_STEP_ROWS, sub_len)
    n_sub = ATTN_STEP_ROWS // qt
    q_spec = pl.BlockSpec((CHUNKS_PER_GROUP, None, n_sub, qt, CHUNK), lambda b, r, t: (0, b, r, t, 0))
    kv_spec = pl.BlockSpec((CHUNKS_PER_GROUP, None, n_sub, sub_len, CHUNK),
                           lambda b, r, t: (0, b, r, 0, 0))
    o_spec = pl.BlockSpec((None, n_sub, qt, GROUP_WIDTH), lambda b, r, t: (b, r, t, 0))
    o_shape = jax.ShapeDtypeStruct((batch, dil, sub_len, GROUP_WIDTH), F32)
    vmem = (2 * CHUNKS_PER_GROUP * ATTN_STEP_ROWS * CHUNK * 2
            + 4 * CHUNKS_PER_GROUP * n_sub * sub_len * CHUNK * 2
            + 4 * ATTN_STEP_ROWS * GROUP_WIDTH * 4 + (8 << 20))
    return pl.pallas_call(
        functools.partial(_attn_body, sub_len, half_window),
        out_shape=[o_shape, o_shape],
        grid=(batch, dil // n_sub, sub_len // qt),
        in_specs=[q_spec, kv_spec, kv_spec],
        out_specs=[o_spec, o_spec],
        compiler_params=_params(3, vmem),
        name=f"attn_d{dil}",
    )(q, k, v)


def _mix_body(conv_width, *refs):
    ol_refs = refs[:2 * N_GROUPS]
    (x_ref, h_ref, hp_ref, hn_ref, w_ref, b_ref,
     wap_ref, cw_ref, cb_ref, lg_ref, lb_ref, wcp_ref, bcp_ref, wo_ref,
     out_ref, ue_scr, cv_scr, gate_scr, nat_scr, tmp_scr) = refs[2 * N_GROUPS:]
    i = pl.program_id(1)
    ts, d = x_ref.shape
    ext = ts + 2 * HALO
    n_ct = d // LANES
    pad = conv_width // 2

    row = lax.broadcasted_iota(jnp.int32, (ext, 1), 0)
    inside = (((row >= HALO) | (i > 0)) & ((row < HALO + ts) | (i < pl.num_programs(1) - 1)))
    h_tile = h_ref[...]
    h_all = jnp.concatenate([hp_ref[...], h_tile, hn_ref[...]], axis=0)

    def proj(h, c):
        cols = slice(c * CHUNK, (c + 1) * CHUNK)
        return jnp.dot(h, w_ref[:, cols], preferred_element_type=F32) + b_ref[:, cols]

    glu_chunks = d // CHUNK
    for c in range(glu_chunks):
        u = jnp.where(inside, proj(h_all, c) * jax.nn.sigmoid(proj(h_all, glu_chunks + c)), 0.0)
        for j in range(TILES_PER_CHUNK):
            ct = c * TILES_PER_CHUNK + j
            ue_scr[ct * ext:(ct + 1) * ext, :] = u[:, j * LANES:(j + 1) * LANES]

    def conv_tile(ct, carry):
        src = ct * ext + (HALO - pad)
        dst = pl.multiple_of(ct * ts, CONV_ROWS)
        for rb in range(ts // CONV_ROWS):
            acc = jnp.broadcast_to(cb_ref[ct], (CONV_ROWS, LANES))
            for j in range(conv_width):
                taps = ue_scr[pl.ds(src + rb * CONV_ROWS + j, CONV_ROWS), :]
                acc = acc + taps * cw_ref[ct, j:j + 1, :]
            cv_scr[pl.ds(dst + rb * CONV_ROWS, CONV_ROWS), :] = acc
        return carry

    lax.fori_loop(0, n_ct, conv_tile, 0)

    for c in range(gate_scr.shape[0]):
        gate_scr[c] = jax.nn.sigmoid(proj(h_tile, 2 * glu_chunks + c))

    lane_tiles = GROUP_WIDTH // LANES
    natural = []
    for n, (ref, dil) in enumerate(zip(ol_refs, DILATIONS + DILATIONS)):
        if dil == 1:
            natural.append(ref[0])
            continue
        slot = (n % N_GROUPS - 1) * 2 + n // N_GROUPS
        fine = min(dil, MAX_ROW_STRIDE)
        coarse = dil // fine
        for j in range(lane_tiles):
            base = (slot * lane_tiles + j) * ts
            lanes = slice(j * LANES, (j + 1) * LANES)
            if coarse == 1:
                for r in range(dil):
                    nat_scr[pl.ds(base + r, ts // dil, stride=dil), :] = ref[r, :, lanes]
                continue
            part = ts // fine
            for a in range(fine):
                for b in range(coarse):
                    tmp_scr[pl.ds(base + a * part + b, ts // dil, stride=coarse), :] = ref[a + fine * b, :, lanes]
            for a in range(fine):
                nat_scr[pl.ds(base + a, part, stride=fine), :] = tmp_scr[base + a * part:base + (a + 1) * part, :]
        natural.append(jnp.concatenate(
            [nat_scr[(slot * lane_tiles + j) * ts:(slot * lane_tiles + j + 1) * ts, :]
             for j in range(lane_tiles)], axis=1))
    o0, o1, o2, l0, l1, l2 = natural
    m = jnp.maximum(jnp.maximum(l0, l1), l2)
    e0, e1, e2 = jnp.exp(l0 - m), jnp.exp(l1 - m), jnp.exp(l2 - m)
    y = (e0 * o0 + e1 * o1 + e2 * o2) / (e0 + e1 + e2)
    y_attn = jnp.dot(y.astype(BF16), wap_ref[...], preferred_element_type=F32)

    conv = [cv_scr[ct * ts:(ct + 1) * ts, :] for ct in range(n_ct)]
    row_sum = jnp.zeros((ts, 1), F32)
    for ct in range(n_ct):
        row_sum = row_sum + jnp.sum(conv[ct], axis=-1, keepdims=True)
    mu = row_sum * (1.0 / d)
    sq_sum = jnp.zeros((ts, 1), F32)
    for ct in range(n_ct):
        dev = conv[ct] - mu
        sq_sum = sq_sum + jnp.sum(dev * dev, axis=-1, keepdims=True)
    rstd = lax.rsqrt(sq_sum * (1.0 / d) + EPS)
    acts = []
    for ct in range(n_ct):
        cols = slice(ct * LANES, (ct + 1) * LANES)
        ln = (conv[ct] - mu) * rstd * lg_ref[:, cols] + lb_ref[:, cols]
        acts.append((ln * jax.nn.sigmoid(ln)).astype(BF16))
    y_conv = jnp.dot(jnp.concatenate(acts, axis=1), wcp_ref[...],
                     preferred_element_type=F32) + bcp_ref[...]

    n_tc = gate_scr.shape[0] // 2
    g_attn = jnp.concatenate([gate_scr[c] for c in range(n_tc)], axis=1)
    g_conv = jnp.concatenate([gate_scr[n_tc + c] for c in range(n_tc)], axis=1)
    merged = (g_attn * y_attn + g_conv * y_conv).astype(BF16)
    out_ref[...] = x_ref[...] + jnp.dot(merged, wo_ref[...], preferred_element_type=F32)


def _mix_out(x, h, outs, lses, layer, consts, conv_width, batch, seq, ts):
    t, d = x.shape
    n_ct = d // LANES
    tiles = seq // ts
    hb = ts // HALO
    n_hb = seq // HALO
    ext = ts + 2 * HALO
    row = lambda b, i: (b * tiles + i, 0)
    ol_specs = [pl.BlockSpec((None, dil, ts // dil, GROUP_WIDTH), lambda b, i: (b, 0, i, 0))
                for dil in DILATIONS]
    hp_spec = pl.BlockSpec((HALO, d), lambda b, i: (b * n_hb + jnp.maximum(i * hb - 1, 0), 0))
    hn_spec = pl.BlockSpec((HALO, d), lambda b, i: (b * n_hb + jnp.minimum((i + 1) * hb, n_hb - 1), 0))
    const_bytes = sum(a.size * a.dtype.itemsize for a in consts) // consts[0].shape[0]
    scratch = [pltpu.VMEM((n_ct * ext, LANES), F32),
               pltpu.VMEM((n_ct * ts, LANES), F32),
               pltpu.VMEM((2 * d // CHUNK, ts, CHUNK), F32),
               pltpu.VMEM((4 * (GROUP_WIDTH // LANES) * ts, LANES), F32),
               pltpu.VMEM((4 * (GROUP_WIDTH // LANES) * ts, LANES), F32)]
    scratch_bytes = (ext + ts) * d * 4 + 2 * ts * d * 4 + 8 * ts * GROUP_WIDTH * 4
    vmem = (2 * 2 * N_GROUPS * ts * GROUP_WIDTH * 4 + 4 * ts * d * 4 + 2 * ext * d * 2
            + const_bytes + scratch_bytes
            + 10 * ts * d * 4 + (4 << 20))
    return pl.pallas_call(
        functools.partial(_mix_body, conv_width),
        out_shape=jax.ShapeDtypeStruct((t, d), F32),
        grid=(batch, tiles),
        in_specs=ol_specs + ol_specs + [pl.BlockSpec((ts, d), row)] * 2 + [hp_spec, hn_spec]
                 + [_layer_spec(a.shape, layer) for a in consts],
        out_specs=pl.BlockSpec((ts, d), row),
        scratch_shapes=scratch,
        compiler_params=_params(2, vmem),
        name="mix_out",
    )(*outs, *lses, x, h, h, h, *consts)


def kernel(x, ffn1_norm, ffn1_w_up, ffn1_w_down, mix_norm, w_in, b_in, q_norm, k_norm,
           w_attn_proj, conv_w, conv_b, conv_ln_g, conv_ln_b, w_conv_proj, b_conv_proj,
           w_out, ffn2_norm, ffn2_w_up, ffn2_w_down, final_norm):
    batch, seq, d = x.shape
    depth = ffn1_norm.shape[0]
    proj_tm = 512
    rope = _rope_tables(seq, proj_tm)

    row = lambda a: a[:, None, :]
    ffn1 = (row(ffn1_norm), ffn1_w_up.astype(BF16), ffn1_w_down.astype(BF16))
    ffn2 = (row(ffn2_norm), ffn2_w_up.astype(BF16), ffn2_w_down.astype(BF16))
    (w_qkv, b_qkv), (w_tail, b_tail) = _split_w_in(w_in, b_in)
    gq, gk = _head_gain(q_norm), _head_gain(k_norm)
    conv_width = conv_w.shape[1]
    n_ct = d // LANES
    cw = jnp.pad(conv_w, ((0, 0), (0, -conv_width % 8), (0, 0)))
    cw = cw.reshape(depth, cw.shape[1], n_ct, LANES).transpose(0, 2, 1, 3)
    mixer_params = [w_tail, b_tail, w_attn_proj.astype(BF16), cw, conv_b.reshape(depth, n_ct, 1, LANES),
                    row(conv_ln_g), row(conv_ln_b), w_conv_proj.astype(BF16), row(b_conv_proj),
                    w_out.astype(BF16)]

    h = x.reshape(batch * seq, d)
    for l in range(depth):
        h = _ffn(h, l, *ffn1, None, tm=1024)
        *qkv, hn = _qkv_proj(h, l, row(mix_norm), w_qkv, b_qkv, gq, gk, rope, batch, seq, proj_tm)
        outs, lses = zip(*[_attn_group(*qkv[3 * g:3 * g + 3], g) for g in range(N_GROUPS)])
        h = _mix_out(h, hn, outs, lses, l, mixer_params, conv_width, batch, seq, ts=512)
        h = _ffn(h, l, *ffn2, row(final_norm), tm=1024)
    return h.reshape(batch, seq, d)
```

```python
import functools
import math

import numpy as np

import jax
import jax.numpy as jnp
from jax import lax
from jax.experimental import pallas as pl
from jax.experimental.pallas import tpu as pltpu

HEAD_DIM = 64
HEADS_PER_GROUP = 8
DILATION_GROUPS = ((128, 1), (512, 4), (2048, 16))
DILATIONS = tuple(dil for _, dil in DILATION_GROUPS)
N_GROUPS = len(DILATION_GROUPS)
GROUP_WIDTH = HEADS_PER_GROUP * HEAD_DIM
ATTN_WIDTH = N_GROUPS * GROUP_WIDTH
ROPE_THETA = 10000.0
EPS = 1e-6
NEG = -1e30

LANES = 128
CHUNK = 256
VMEM_BYTES = 64 * 1024 * 1024
HALO = 16
Q_BLOCK = 128
UNROLLED_Q_BLOCKS = 16
ATTN_STEP_ROWS = 2048
CONV_ROWS = 128
MAX_ROW_STRIDE = 4
HEADS_PER_CHUNK = CHUNK // HEAD_DIM
CHUNKS_PER_GROUP = GROUP_WIDTH // CHUNK
TILES_PER_CHUNK = CHUNK // LANES
HALF = HEAD_DIM // 2

F32 = jnp.float32
BF16 = jnp.bfloat16


def _rms(x, g):
    return x * lax.rsqrt(jnp.mean(x * x, axis=-1, keepdims=True) + EPS) * g


def _const_spec(shape):
    zeros = (0,) * len(shape)
    return pl.BlockSpec(shape, lambda *_: zeros, pipeline_mode=pl.Buffered(1))


def _layer_spec(stacked_shape, layer):
    index = (layer,) + (0,) * (len(stacked_shape) - 1)
    return pl.BlockSpec((None,) + tuple(stacked_shape[1:]), lambda *_: index,
                        pipeline_mode=pl.Buffered(1))


def _params(n_axes, vmem_bytes):
    return pltpu.CompilerParams(
        dimension_semantics=("arbitrary",) * n_axes,
        vmem_limit_bytes=min(int(vmem_bytes), VMEM_BYTES - 4 * 1024 * 1024))


def _ffn_body(final_norm, x_ref, g_ref, wup_ref, wdn_ref, *rest):
    if final_norm:
        fg_ref, o_ref, h_scr, acc_scr = rest
    else:
        o_ref, h_scr, acc_scr = rest
    d_ff = wdn_ref.shape[0]
    h_scr[...] = _rms(x_ref[...], g_ref[...]).astype(BF16)
    for c in range(d_ff // CHUNK):
        h = h_scr[...]
        a = jnp.dot(h, wup_ref[:, c * CHUNK:(c + 1) * CHUNK], preferred_element_type=F32)
        b = jnp.dot(h, wup_ref[:, d_ff + c * CHUNK:d_ff + (c + 1) * CHUNK], preferred_element_type=F32)
        s = (a * jax.nn.sigmoid(a) * b).astype(BF16)
        y = jnp.dot(s, wdn_ref[c * CHUNK:(c + 1) * CHUNK, :], preferred_element_type=F32)
        if c == 0:
            acc_scr[...] = y
        else:
            acc_scr[...] += y
    y = x_ref[...] + 0.5 * acc_scr[...]
    if final_norm:
        y = _rms(y, fg_ref[...])
    o_ref[...] = y


def _ffn(x, layer, norm_g, w_up, w_down, final_g, tm):
    t, d = x.shape
    d_ff = w_down.shape[1]
    final_norm = final_g is not None
    args = [x, norm_g, w_up, w_down] + ([final_g] if final_norm else [])
    specs = ([pl.BlockSpec((tm, d), lambda i: (i, 0))]
             + [_layer_spec(a.shape, layer) for a in args[1:]])
    vmem = (4 * tm * d * 4 + 3 * d_ff * d * 2 + tm * d * 6 + 8 * tm * CHUNK * 4 + (8 << 20))
    return pl.pallas_call(
        functools.partial(_ffn_body, final_norm),
        out_shape=jax.ShapeDtypeStruct((t, d), F32),
        grid=(t // tm,),
        in_specs=specs,
        out_specs=pl.BlockSpec((tm, d), lambda i: (i, 0)),
        scratch_shapes=[pltpu.VMEM((tm, d), BF16), pltpu.VMEM((tm, d), F32)],
        compiler_params=_params(1, vmem),
        name="ffn_final" if final_norm else "ffn",
    )(*args)


def _rotary_cols(a):
    lead = a.shape[:-1]
    a = a.reshape(*lead, -1, HEADS_PER_CHUNK, TILES_PER_CHUNK, HALF)
    return jnp.swapaxes(a, -2, -3).reshape(*lead, -1)


def _split_w_in(w_in, b_in):
    def qkv(a):
        return jnp.concatenate([_rotary_cols(a[..., :ATTN_WIDTH]),
                                _rotary_cols(a[..., ATTN_WIDTH:2 * ATTN_WIDTH]),
                                a[..., 2 * ATTN_WIDTH:3 * ATTN_WIDTH]], axis=-1)
    b_in = b_in[:, None, :]
    return ((qkv(w_in).astype(BF16), qkv(b_in)),
            (w_in[..., 3 * ATTN_WIDTH:].astype(BF16), b_in[..., 3 * ATTN_WIDTH:]))


def _head_gain(g):
    layers = g.shape[0]
    full = jnp.broadcast_to(g[:, :, None, :], (layers, N_GROUPS, HEADS_PER_GROUP, HEAD_DIM))
    return _rotary_cols(full.reshape(layers, 1, ATTN_WIDTH))


def _rope_tables(seq, tm):
    inv = ROPE_THETA ** (-jnp.arange(0, HEAD_DIM, 2, dtype=F32) / HEAD_DIM)
    tables = []
    for dil in DILATIONS:
        row = np.arange(seq)
        tile, r, m = row // tm, (row % tm) // (tm // dil), row % (tm // dil)
        pos = jnp.asarray(tile * tm + m * dil + r, F32)
        ang = pos[:, None] * inv[None, :]
        cos = jnp.tile(jnp.cos(ang), (1, LANES // HALF))
        sin = jnp.tile(jnp.sin(ang), (1, LANES // HALF))
        scale = 1.0 / math.sqrt(HEAD_DIM)
        tables.append(jnp.stack([cos * scale, sin * scale, cos, sin]))
    return jnp.stack(tables)


def _segment_matrix():
    lane = np.arange(LANES)
    same = (lane[:, None] // HALF == lane[None, :] // HALF).astype(np.float32)
    return jnp.asarray(np.tile(same, (2, TILES_PER_CHUNK)), BF16)


def _head_norm_rope(z, seg_ref, gain, cos, sin):
    z0, z1 = z[:, :LANES], z[:, LANES:]
    w = z0 * z0 + z1 * z1
    w_hi = w.astype(BF16)
    w_lo = (w - w_hi.astype(F32)).astype(BF16)
    ss = jnp.dot(jnp.concatenate([w_hi, w_lo], axis=1), seg_ref[...], preferred_element_type=F32)
    t = z * lax.rsqrt(ss * (1.0 / HEAD_DIM) + EPS) * gain
    t0, t1 = t[:, :LANES], t[:, LANES:]
    return jnp.concatenate([t0 * cos - t1 * sin, t1 * cos + t0 * sin], axis=1)


def _qkv_body(x_ref, g_ref, w_ref, b_ref, gq_ref, gk_ref, rope_ref, seg_ref, *rest):
    qkv_refs = rest[:3 * N_GROUPS]
    h_ref, hn_scr = rest[3 * N_GROUPS:3 * N_GROUPS + 2]
    h_scrs = (h_ref,) + rest[3 * N_GROUPS + 2:]
    tm = x_ref.shape[0]

    hn = _rms(x_ref[...], g_ref[...])
    h_ref[...] = hn.astype(BF16)
    lane_tiles = hn_scr.shape[0] // tm
    for j in range(lane_tiles):
        hn_scr[j * tm:(j + 1) * tm, :] = hn[:, j * LANES:(j + 1) * LANES]
    for dil, h_scr in zip(DILATIONS[1:], h_scrs[1:]):
        rows = tm // dil
        for r in range(dil):
            for j in range(lane_tiles):
                src = hn_scr[pl.ds(j * tm + r, rows, stride=dil), :]
                h_scr[r * rows:(r + 1) * rows, j * LANES:(j + 1) * LANES] = src.astype(BF16)

    for g, dil in enumerate(DILATIONS):
        q_ref, k_ref, v_ref = qkv_refs[3 * g:3 * g + 3]
        h = h_scrs[g][...]
        shape = (dil, tm // dil, CHUNK)
        for c in range(CHUNKS_PER_GROUP):
            cols = slice((g * CHUNKS_PER_GROUP + c) * CHUNK, (g * CHUNKS_PER_GROUP + c + 1) * CHUNK)
            z = [jnp.dot(h, w_ref[:, n * ATTN_WIDTH + cols.start:n * ATTN_WIDTH + cols.stop],
                         preferred_element_type=F32)
                 + b_ref[:, n * ATTN_WIDTH + cols.start:n * ATTN_WIDTH + cols.stop] for n in range(3)]
            q = _head_norm_rope(z[0], seg_ref, gq_ref[:, cols], rope_ref[g, 0], rope_ref[g, 1])
            q_ref[c] = q.astype(BF16).reshape(shape)
            k = _head_norm_rope(z[1], seg_ref, gk_ref[:, cols], rope_ref[g, 2], rope_ref[g, 3])
            k_ref[c] = k.astype(BF16).reshape(shape)
            v_ref[c] = z[2].astype(BF16).reshape(shape)


def _qkv_proj(x, layer, mix_g, w_qkv, b_qkv, gq, gk, rope, batch, seq, tm):
    t, d = x.shape
    seg = _segment_matrix()
    tiles = seq // tm
    consts = [mix_g, w_qkv, b_qkv, gq, gk]
    args = [x, *consts, rope, seg]
    specs = ([pl.BlockSpec((tm, d), lambda i, b: (b * tiles + i, 0))]
             + [_layer_spec(a.shape, layer) for a in consts]
             + [pl.BlockSpec((N_GROUPS, 4, tm, LANES), lambda i, b: (0, 0, i, 0)),
                _const_spec(seg.shape)])
    out_shape, out_specs = [], []
    for dil in DILATIONS:
        shape = (CHUNKS_PER_GROUP, batch, dil, seq // dil, CHUNK)
        spec = pl.BlockSpec((CHUNKS_PER_GROUP, None, dil, tm // dil, CHUNK),
                            lambda i, b: (0, b, 0, i, 0))
        out_shape += [jax.ShapeDtypeStruct(shape, BF16)] * 3
        out_specs += [spec] * 3
    out_shape.append(jax.ShapeDtypeStruct((t, d), BF16))
    out_specs.append(pl.BlockSpec((tm, d), lambda i, b: (b * tiles + i, 0)))
    vmem = (2 * tm * d * 4 + 3 * ATTN_WIDTH * d * 2 + 2 * 3 * ATTN_WIDTH * tm * 2
            + tm * d * (4 + 2 * N_GROUPS) + 2 * N_GROUPS * 4 * tm * LANES * 4
            + 8 * tm * CHUNK * 4 + (6 << 20))
    return pl.pallas_call(
        _qkv_body,
        out_shape=out_shape,
        grid=(tiles, batch),
        in_specs=specs,
        out_specs=out_specs,
        scratch_shapes=[pltpu.VMEM((d // LANES * tm, LANES), F32)] + [pltpu.VMEM((tm, d), BF16)] * (N_GROUPS - 1),
        compiler_params=_params(2, vmem),
        name="qkv_proj",
    )(*args)


def _attn_body(sub_len, half_window, q_ref, k_ref, v_ref, o_ref, l_ref):
    n_sub, qt_rows = q_ref.shape[1:3]
    blocks_per_sub = qt_rows // Q_BLOCK
    key_rows = Q_BLOCK + 2 * half_window
    tile_start = pl.program_id(2) * qt_rows
    lane = lax.broadcasted_iota(jnp.int32, (1, CHUNK), 1)
    head_lanes = [((lane % LANES) // HALF == hh).astype(BF16) for hh in range(HEADS_PER_CHUNK)]
    v_head = lax.broadcasted_iota(jnp.int32, (Q_BLOCK, CHUNK), 1) // HEAD_DIM
    offset = (lax.broadcasted_iota(jnp.int32, (Q_BLOCK, key_rows), 0)
              - lax.broadcasted_iota(jnp.int32, (Q_BLOCK, key_rows), 1))

    def block(n, carry):
        sub = n // blocks_per_sub
        r0 = pl.multiple_of((n % blocks_per_sub) * Q_BLOCK, Q_BLOCK)
        q0 = tile_start + r0
        start = pl.multiple_of(jnp.clip(q0 - half_window, 0, sub_len - key_rows), half_window)
        band = jnp.where(jnp.abs(offset + (q0 - start)) <= half_window, 0.0, NEG)
        for c in range(CHUNKS_PER_GROUP):
            qb = q_ref[c, sub, pl.ds(r0, Q_BLOCK), :]
            kw = k_ref[c, sub, pl.ds(start, key_rows), :]
            vw = v_ref[c, sub, pl.ds(start, key_rows), :]
            qs = jnp.concatenate([qb * head_lanes[hh] for hh in range(HEADS_PER_CHUNK)], axis=0)
            s = lax.dot_general(qs, kw, (((1,), (1,)), ((), ())), preferred_element_type=F32)
            ps, maxes, dens = [], [], []
            for hh in range(HEADS_PER_CHUNK):
                sh = s[hh * Q_BLOCK:(hh + 1) * Q_BLOCK] + band
                m = jnp.max(sh, axis=-1, keepdims=True)
                p = jnp.exp(sh - m)
                ps.append(p.astype(BF16))
                maxes.append(m)
                dens.append(jnp.sum(p, axis=-1, keepdims=True))
            pv = jnp.dot(jnp.concatenate(ps, axis=0), vw, preferred_element_type=F32)
            o, mx, dn = pv[:Q_BLOCK], maxes[0], dens[0]
            for hh in range(1, HEADS_PER_CHUNK):
                mine = v_head == hh
                o = jnp.where(mine, pv[hh * Q_BLOCK:(hh + 1) * Q_BLOCK], o)
                mx = jnp.where(mine, maxes[hh], mx)
                dn = jnp.where(mine, dens[hh], dn)
            cols = slice(c * CHUNK, (c + 1) * CHUNK)
            o_ref[sub, pl.ds(r0, Q_BLOCK), cols] = o / dn
            l_ref[sub, pl.ds(r0, Q_BLOCK), cols] = mx + jnp.log(dn)
        return carry

    lax.fori_loop(0, n_sub * blocks_per_sub, block, 0, unroll=UNROLLED_Q_BLOCKS)


def _attn_group(q, k, v, g):
    window, dil = DILATION_GROUPS[g]
    half_window = window // (2 * dil)
    _, batch, _, sub_len, _ = q.shape
    qt = min(ATTN_STEP_ROWS, sub_len)
    n_sub = ATTN_STEP_ROWS // qt
    q_spec = pl.BlockSpec((CHUNKS_PER_GROUP, None, n_sub, qt, CHUNK), lambda b, r, t: (0, b, r, t, 0))
    kv_spec = pl.BlockSpec((CHUNKS_PER_GROUP, None, n_sub, sub_len, CHUNK),
                           lambda b, r, t: (0, b, r, 0, 0))
    o_spec = pl.BlockSpec((None, n_sub, qt, GROUP_WIDTH), lambda b, r, t: (b, r, t, 0))
    o_shape = jax.ShapeDtypeStruct((batch, dil, sub_len, GROUP_WIDTH), F32)
    vmem = (2 * CHUNKS_PER_GROUP * ATTN_STEP_ROWS * CHUNK * 2
            + 4 * CHUNKS_PER_GROUP * n_sub * sub_len * CHUNK * 2
            + 4 * ATTN_STEP_ROWS * GROUP_WIDTH * 4 + (8 << 20))
    return pl.pallas_call(
        functools.partial(_attn_body, sub_len, half_window),
        out_shape=[o_shape, o_shape],
        grid=(batch, dil // n_sub, sub_len // qt),
        in_specs=[q_spec, kv_spec, kv_spec],
        out_specs=[o_spec, o_spec],
        compiler_params=_params(3, vmem),
        name=f"attn_d{dil}",
    )(q, k, v)


def _mix_body(conv_width, *refs):
    ol_refs = refs[:2 * N_GROUPS]
    (x_ref, h_ref, hp_ref, hn_ref, w_ref, b_ref,
     wap_ref, cw_ref, cb_ref, lg_ref, lb_ref, wcp_ref, bcp_ref, wo_ref,
     out_ref, ue_scr, cv_scr, gate_scr, nat_scr, tmp_scr) = refs[2 * N_GROUPS:]
    i = pl.program_id(1)
    ts, d = x_ref.shape
    ext = ts + 2 * HALO
    n_ct = d // LANES
    pad = conv_width // 2

    row = lax.broadcasted_iota(jnp.int32, (ext, 1), 0)
    inside = (((row >= HALO) | (i > 0)) & ((row < HALO + ts) | (i < pl.num_programs(1) - 1)))
    h_tile = h_ref[...]
    h_all = jnp.concatenate([hp_ref[...], h_tile, hn_ref[...]], axis=0)

    def proj(h, c):
        cols = slice(c * CHUNK, (c + 1) * CHUNK)
        return jnp.dot(h, w_ref[:, cols], preferred_element_type=F32) + b_ref[:, cols]

    glu_chunks = d // CHUNK
    for c in range(glu_chunks):
        u = jnp.where(inside, proj(h_all, c) * jax.nn.sigmoid(proj(h_all, glu_chunks + c)), 0.0)
        for j in range(TILES_PER_CHUNK):
            ct = c * TILES_PER_CHUNK + j
            ue_scr[ct * ext:(ct + 1) * ext, :] = u[:, j * LANES:(j + 1) * LANES]

    def conv_tile(ct, carry):
        src = ct * ext + (HALO - pad)
        dst = pl.multiple_of(ct * ts, CONV_ROWS)
        for rb in range(ts // CONV_ROWS):
            acc = jnp.broadcast_to(cb_ref[ct], (CONV_ROWS, LANES))
            for j in range(conv_width):
                taps = ue_scr[pl.ds(src + rb * CONV_ROWS + j, CONV_ROWS), :]
                acc = acc + taps * cw_ref[ct, j:j + 1, :]
            cv_scr[pl.ds(dst + rb * CONV_ROWS, CONV_ROWS), :] = acc
        return carry

    lax.fori_loop(0, n_ct, conv_tile, 0)

    for c in range(gate_scr.shape[0]):
        gate_scr[c] = jax.nn.sigmoid(proj(h_tile, 2 * glu_chunks + c))

    lane_tiles = GROUP_WIDTH // LANES
    natural = []
    for n, (ref, dil) in enumerate(zip(ol_refs, DILATIONS + DILATIONS)):
        if dil == 1:
            natural.append(ref[0])
            continue
        slot = (n % N_GROUPS - 1) * 2 + n // N_GROUPS
        fine = min(dil, MAX_ROW_STRIDE)
        coarse = dil // fine
        for j in range(lane_tiles):
            base = (slot * lane_tiles + j) * ts
            lanes = slice(j * LANES, (j + 1) * LANES)
            if coarse == 1:
                for r in range(dil):
                    nat_scr[pl.ds(base + r, ts // dil, stride=dil), :] = ref[r, :, lanes]
                continue
            part = ts // fine
            for a in range(fine):
                for b in range(coarse):
                    tmp_scr[pl.ds(base + a * part + b, ts // dil, stride=coarse), :] = ref[a + fine * b, :, lanes]
            for a in range(fine):
                nat_scr[pl.ds(base + a, part, stride=fine), :] = tmp_scr[base + a * part:base + (a + 1) * part, :]
        natural.append(jnp.concatenate(
            [nat_scr[(slot * lane_tiles + j) * ts:(slot * lane_tiles + j + 1) * ts, :]
             for j in range(lane_tiles)], axis=1))
    o0, o1, o2, l0, l1, l2 = natural
    m = jnp.maximum(jnp.maximum(l0, l1), l2)
    e0, e1, e2 = jnp.exp(l0 - m), jnp.exp(l1 - m), jnp.exp(l2 - m)
    y = (e0 * o0 + e1 * o1 + e2 * o2) / (e0 + e1 + e2)
    y_attn = jnp.dot(y.astype(BF16), wap_ref[...], preferred_element_type=F32)

    conv = [cv_scr[ct * ts:(ct + 1) * ts, :] for ct in range(n_ct)]
    row_sum = jnp.zeros((ts, 1), F32)
    for ct in range(n_ct):
        row_sum = row_sum + jnp.sum(conv[ct], axis=-1, keepdims=True)
    mu = row_sum * (1.0 / d)
    sq_sum = jnp.zeros((ts, 1), F32)
    for ct in range(n_ct):
        dev = conv[ct] - mu
        sq_sum = sq_sum + jnp.sum(dev * dev, axis=-1, keepdims=True)
    rstd = lax.rsqrt(sq_sum * (1.0 / d) + EPS)
    acts = []
    for ct in range(n_ct):
        cols = slice(ct * LANES, (ct + 1) * LANES)
        ln = (conv[ct] - mu) * rstd * lg_ref[:, cols] + lb_ref[:, cols]
        acts.append((ln * jax.nn.sigmoid(ln)).astype(BF16))
    y_conv = jnp.dot(jnp.concatenate(acts, axis=1), wcp_ref[...],
                     preferred_element_type=F32) + bcp_ref[...]

    n_tc = gate_scr.shape[0] // 2
    g_attn = jnp.concatenate([gate_scr[c] for c in range(n_tc)], axis=1)
    g_conv = jnp.concatenate([gate_scr[n_tc + c] for c in range(n_tc)], axis=1)
    merged = (g_attn * y_attn + g_conv * y_conv).astype(BF16)
    out_ref[...] = x_ref[...] + jnp.dot(merged, wo_ref[...], preferred_element_type=F32)


def _mix_out(x, h, outs, lses, layer, consts, conv_width, batch, seq, ts):
    t, d = x.shape
    n_ct = d // LANES
    tiles = seq // ts
    hb = ts // HALO
    n_hb = seq // HALO
    ext = ts + 2 * HALO
    row = lambda b, i: (b * tiles + i, 0)
    ol_specs = [pl.BlockSpec((None, dil, ts // dil, GROUP_WIDTH), lambda b, i: (b, 0, i, 0))
                for dil in DILATIONS]
    hp_spec = pl.BlockSpec((HALO, d), lambda b, i: (b * n_hb + jnp.maximum(i * hb - 1, 0), 0))
    hn_spec = pl.BlockSpec((HALO, d), lambda b, i: (b * n_hb + jnp.minimum((i + 1) * hb, n_hb - 1), 0))
    const_bytes = sum(a.size * a.dtype.itemsize for a in consts) // consts[0].shape[0]
    scratch = [pltpu.VMEM((n_ct * ext, LANES), F32),
               pltpu.VMEM((n_ct * ts, LANES), F32),
               pltpu.VMEM((2 * d // CHUNK, ts, CHUNK), F32),
               pltpu.VMEM((4 * (GROUP_WIDTH // LANES) * ts, LANES), F32),
               pltpu.VMEM((4 * (GROUP_WIDTH // LANES) * ts, LANES), F32)]
    scratch_bytes = (ext + ts) * d * 4 + 2 * ts * d * 4 + 8 * ts * GROUP_WIDTH * 4
    vmem = (2 * 2 * N_GROUPS * ts * GROUP_WIDTH * 4 + 4 * ts * d * 4 + 2 * ext * d * 2
            + const_bytes + scratch_bytes
            + 10 * ts * d * 4 + (4 << 20))
    return pl.pallas_call(
        functools.partial(_mix_body, conv_width),
        out_shape=jax.ShapeDtypeStruct((t, d), F32),
        grid=(batch, tiles),
        in_specs=ol_specs + ol_specs + [pl.BlockSpec((ts, d), row)] * 2 + [hp_spec, hn_spec]
                 + [_layer_spec(a.shape, layer) for a in consts],
        out_specs=pl.BlockSpec((ts, d), row),
        scratch_shapes=scratch,
        compiler_params=_params(2, vmem),
        name="mix_out",
    )(*outs, *lses, x, h, h, h, *consts)


def kernel(x, ffn1_norm, ffn1_w_up, ffn1_w_down, mix_norm, w_in, b_in, q_norm, k_norm,
           w_attn_proj, conv_w, conv_b, conv_ln_g, conv_ln_b, w_conv_proj, b_conv_proj,
           w_out, ffn2_norm, ffn2_w_up, ffn2_w_down, final_norm):
    batch, seq, d = x.shape
    depth = ffn1_norm.shape[0]
    proj_tm = 512
    rope = _rope_tables(seq, proj_tm)

    row = lambda a: a[:, None, :]
    ffn1 = (row(ffn1_norm), ffn1_w_up.astype(BF16), ffn1_w_down.astype(BF16))
    ffn2 = (row(ffn2_norm), ffn2_w_up.astype(BF16), ffn2_w_down.astype(BF16))
    (w_qkv, b_qkv), (w_tail, b_tail) = _split_w_in(w_in, b_in)
    gq, gk = _head_gain(q_norm), _head_gain(k_norm)
    conv_width = conv_w.shape[1]
    n_ct = d // LANES
    cw = jnp.pad(conv_w, ((0, 0), (0, -conv_width % 8), (0, 0)))
    cw = cw.reshape(depth, cw.shape[1], n_ct, LANES).transpose(0, 2, 1, 3)
    mixer_params = [w_tail, b_tail, w_attn_proj.astype(BF16), cw, conv_b.reshape(depth, n_ct, 1, LANES),
                    row(conv_ln_g), row(conv_ln_b), w_conv_proj.astype(BF16), row(b_conv_proj),
                    w_out.astype(BF16)]

    h = x.reshape(batch * seq, d)
    for l in range(depth):
        h = _ffn(h, l, *ffn1, None, tm=1024)
        *qkv, hn = _qkv_proj(h, l, row(mix_norm), w_qkv, b_qkv, gq, gk, rope, batch, seq, proj_tm)
        outs, lses = zip(*[_attn_group(*qkv[3 * g:3 * g + 3], g) for g in range(N_GROUPS)])
        h = _mix_out(h, hn, outs, lses, l, mixer_params, conv_width, batch, seq, ts=512)
        h = _ffn(h, l, *ffn2, row(final_norm), tm=1024)
    return h.reshape(batch, seq, d)
```
